```python
import jax, jax.numpy as jnp
from jax import lax
import numpy as np

D_MODEL = 1024
BATCH = 4
SEQ = 4096
DEPTH = 1
DEC_BATCH = 2
DEC_SEQ = 8192
PAST_LEN = 128

GRID_W = 64
HGRN_HEADS = 4
HGRN_DK = 128
HGRN_DV = 128
HGRN_WIDTH = HGRN_HEADS * HGRN_DV
CHUNK = 64
NA_HEADS = 8
NA_DH = 64
NA_WIDTH = NA_HEADS * NA_DH
NA_KH_MAX = 8
NA_KW = 16
NA_QBLK = 16
NA_KBAND = 32
D_FF = 2816
EPS = 1e-6
IN_COLS = 5 * HGRN_WIDTH + 3 * NA_WIDTH
SPLITS = [HGRN_WIDTH, 2 * HGRN_WIDTH, 3 * HGRN_WIDTH, 4 * HGRN_WIDTH, 5 * HGRN_WIDTH,
          5 * HGRN_WIDTH + NA_WIDTH, 5 * HGRN_WIDTH + 2 * NA_WIDTH]

kernel_name = "hymba_hgrn2_natten_encoder"


def rmsnorm(x, w):
    xf = x.astype(jnp.float32)
    y = xf * lax.rsqrt(jnp.mean(xf * xf, axis=-1, keepdims=True) + EPS) * w.astype(jnp.float32)
    return y.astype(x.dtype)


def hgrn2_direction(q, k, log_f, v):
    B, T, H, dk = q.shape
    dv = v.shape[-1]
    n = T // CHUNK

    def chunks(a):
        return a.reshape(B, n, CHUNK, H, a.shape[-1]).transpose(1, 0, 3, 2, 4)

    tril = jnp.tril(jnp.ones((CHUNK, CHUNK), dtype=bool))[..., None]

    def step(S, inp):
        qc, kc, gc, vc = inp
        b = jnp.cumsum(gc, axis=2)
        diff = b[:, :, :, None, :] - b[:, :, None, :, :]
        decay = jnp.exp(jnp.where(tril, diff, -jnp.inf))
        A = jnp.einsum('bhtd,bhsd,bhtsd->bhts', qc, kc, decay)
        b_last = b[:, :, -1:, :]
        o = jnp.einsum('bhts,bhse->bhte', A, vc) + jnp.einsum('bhtd,bhde->bhte', qc * jnp.exp(b), S)
        S = jnp.exp(b_last[:, :, 0, :])[..., None] * S + jnp.einsum(
            'bhsd,bhse->bhde', kc * jnp.exp(b_last - b), vc)
        return S, o

    S0 = jnp.zeros((B, H, dk, dv), jnp.float32)
    _, o = lax.scan(step, S0, (chunks(q), chunks(k), chunks(log_f), chunks(v)))
    return o.transpose(1, 0, 3, 2, 4).reshape(B, T, H, dv)


def neighbourhood_attention(q, k, v, rpb):
    B, T = q.shape[0], q.shape[1]
    rows = T // GRID_W
    kh = min(NA_KH_MAX, rows)
    ncb = GRID_W // NA_QBLK
    r = np.arange(rows)
    row_start = np.clip(r - kh // 2, 0, rows - kh)
    row_idx = row_start[:, None] + np.arange(kh)[None, :]
    j = np.arange(ncb)
    band_start = np.clip(j * NA_QBLK - NA_KW // 2, 0, GRID_W - NA_KBAND)
    col_idx = band_start[:, None] + np.arange(NA_KBAND)[None, :]
    qcol = j[:, None] * NA_QBLK + np.arange(NA_QBLK)[None, :]
    win_start = np.clip(qcol - NA_KW // 2, 0, GRID_W - NA_KW)
    valid = ((col_idx[:, None, :] >= win_start[:, :, None]) &
             (col_idx[:, None, :] < win_start[:, :, None] + NA_KW))
    dr = row_idx - r[:, None]
    dc = np.clip(col_idx[:, None, :] - qcol[:, :, None], -(NA_KW - 1), NA_KW - 1)

    def grid(a):
        return a.reshape(B, rows, GRID_W, NA_HEADS, NA_DH).transpose(0, 3, 1, 2, 4)

    qg, kg, vg = grid(q), grid(k), grid(v)
    qb = qg.reshape(B, NA_HEADS, rows, ncb, NA_QBLK, NA_DH)
    ri = row_idx[:, None, :, None]
    ci = col_idx[None, :, None, :]
    kb = kg[:, :, ri, ci]
    vb = vg[:, :, ri, ci]
    scores = jnp.einsum('bhrjqd,bhrjkwd->bhrjqkw', qb, kb).astype(jnp.float32) * (NA_DH ** -0.5)
    bias = rpb[:, (dr + NA_KH_MAX - 1)[:, None, None, :, None],
               (dc + NA_KW - 1)[None, :, :, None, :]]
    scores = scores + bias[None].astype(jnp.float32)
    scores = jnp.where(valid[None, None, None, :, :, None, :], scores, -1e30)
    shp = scores.shape
    p = jax.nn.softmax(scores.reshape(shp[:-2] + (kh * NA_KBAND,)), axis=-1).reshape(shp)
    out = jnp.einsum('bhrjqkw,bhrjkwd->bhrjqd', p.astype(vb.dtype), vb)
    out = out.reshape(B, NA_HEADS, rows, GRID_W, NA_DH).transpose(0, 2, 3, 1, 4)
    return out.reshape(B, T, NA_WIDTH)


def encoder_layer(x, norm_mix_w, w_in, lb, hgrn_gnorm_w, na_q_norm_w, na_k_norm_w, na_rpb,
                  w_out, norm_ffn_w, w_gate, w_up, w_down):
    B, T, _ = x.shape
    f32 = jnp.float32
    h = rmsnorm(x, norm_mix_w)
    proj = h @ w_in
    hq, hf_fwd, hf_bwd, hi, hg, nq, nk, nv = jnp.split(proj, SPLITS, axis=-1)

    def heads(a, d):
        return a.reshape(B, T, -1, d)

    qh = heads(jax.nn.silu(hq.astype(f32)), HGRN_DK)
    vh = heads(hi.astype(f32), HGRN_DV)

    def gate(logit, lb_d):
        f = lb_d + (1.0 - lb_d) * jax.nn.sigmoid(logit.astype(f32))
        return heads(1.0 - f, HGRN_DK), heads(jnp.log(f), HGRN_DK)

    kf, gf = gate(hf_fwd, lb[0])
    kbk, gbk = gate(hf_bwd, lb[1])
    o_f = hgrn2_direction(qh, kf, gf, vh)
    o_b = jnp.flip(hgrn2_direction(jnp.flip(qh, 1), jnp.flip(kbk, 1), jnp.flip(gbk, 1),
                                   jnp.flip(vh, 1)), axis=1)
    o = o_f + o_b
    o = (o * lax.rsqrt(jnp.mean(o * o, axis=-1, keepdims=True) + EPS) * hgrn_gnorm_w.astype(f32)
         * jax.nn.silu(heads(hg.astype(f32), HGRN_DV)))
    o_hgrn = o.reshape(B, T, HGRN_WIDTH).astype(x.dtype)

    qa = rmsnorm(heads(nq, NA_DH), na_q_norm_w)
    ka = rmsnorm(heads(nk, NA_DH), na_k_norm_w)
    va = heads(nv, NA_DH)
    o_na = neighbourhood_attention(qa, ka, va, na_rpb).astype(x.dtype)

    x = x + jnp.concatenate([o_hgrn, o_na], axis=-1) @ w_out

    h2 = rmsnorm(x, norm_ffn_w)
    x = x + (jax.nn.silu(h2 @ w_gate) * (h2 @ w_up)) @ w_down
    return x


def setup_inputs(seed: int = 0) -> dict:
    key = jax.random.key(seed)
    ks = jax.random.split(key, 14)
    n = jax.random.normal
    return {
        "x_prompt": n(ks[0], (BATCH, SEQ, D_MODEL), jnp.float32),
        "x_sample": n(ks[1], (DEC_BATCH, DEC_SEQ, D_MODEL), jnp.float32),
        "norm_mix_w": 1.0 + 0.1 * n(ks[2], (DEPTH, D_MODEL), jnp.float32),
        "w_in": n(ks[3], (DEPTH, D_MODEL, IN_COLS), jnp.float32) * D_MODEL ** -0.5,
        "hgrn_lb": 0.5 * n(ks[4], (DEPTH + 1, 2, HGRN_WIDTH), jnp.float32),
        "hgrn_gnorm_w": 1.0 + 0.1 * n(ks[5], (DEPTH, HGRN_DV), jnp.float32),
        "na_q_norm_w": 1.0 + 0.1 * n(ks[6], (DEPTH, NA_DH), jnp.float32),
        "na_k_norm_w": 1.0 + 0.1 * n(ks[7], (DEPTH, NA_DH), jnp.float32),
        "na_rpb": 0.1 * n(ks[8], (DEPTH, NA_HEADS, 2 * NA_KH_MAX - 1, 2 * NA_KW - 1), jnp.float32),
        "w_out": n(ks[9], (DEPTH, D_MODEL, D_MODEL), jnp.float32) * D_MODEL ** -0.5,
        "norm_ffn_w": 1.0 + 0.1 * n(ks[10], (DEPTH, D_MODEL), jnp.float32),
        "w_gate": n(ks[11], (DEPTH, D_MODEL, D_FF), jnp.float32) * D_MODEL ** -0.5,
        "w_up": n(ks[12], (DEPTH, D_MODEL, D_FF), jnp.float32) * D_MODEL ** -0.5,
        "w_down": n(ks[13], (DEPTH, D_FF, D_MODEL), jnp.float32) * D_FF ** -0.5,
    }


def reference(x_prompt, x_sample, norm_mix_w, w_in, hgrn_lb, hgrn_gnorm_w, na_q_norm_w,
              na_k_norm_w, na_rpb, w_out, norm_ffn_w, w_gate, w_up, w_down):
    lb_all = jnp.cumsum(jax.nn.softmax(hgrn_lb.astype(jnp.float32), axis=0), axis=0)

    def trunk(x):
        for l in range(DEPTH):
            x = encoder_layer(x, norm_mix_w[l], w_in[l], lb_all[l], hgrn_gnorm_w[l], na_q_norm_w[l],
                              na_k_norm_w[l], na_rpb[l], w_out[l], norm_ffn_w[l], w_gate[l],
                              w_up[l], w_down[l])
        return x

    y_prompt = trunk(x_prompt)
    y_sample = trunk(x_sample)
    return (y_prompt, y_sample)
```

```python
import functools

import numpy as np
import jax
import jax.numpy as jnp
from jax import lax
from jax.experimental import pallas as pl
from jax.experimental.pallas import tpu as pltpu

F32 = jnp.float32
BF16 = jnp.bfloat16

D_MODEL = 1024
GRID_W = 64
HG_HEADS = 4
HG_D = 128
HG_W = HG_HEADS * HG_D
CHUNK = 64
SUB = 16
NA_HEADS = 8
NA_DH = 64
NA_W = NA_HEADS * NA_DH
NA_KH = 8
NA_KW = 16
D_FF = 2816
EPS = 1e-6
MASK_VALUE = -1e30

TOKEN_TILE = 512
NA_TILE_ROWS = 8
VMEM_LIMIT_BYTES = 56 * 1024 * 1024


def _silu(x):
    return x * jax.nn.sigmoid(x)


def _const_spec(shape):
    return pl.BlockSpec(shape, lambda *_: (0,) * len(shape), pipeline_mode=pl.Buffered(1))


def _chunk_scan(g, rowmod, reverse):
    n = g.shape[0]
    k = 1
    while k < CHUNK:
        if reverse:
            shifted = pltpu.roll(g, n - k, 0)
            g = g + jnp.where(rowmod < CHUNK - k, shifted, 0.0)
        else:
            shifted = pltpu.roll(g, k, 0)
            g = g + jnp.where(rowmod >= k, shifted, 0.0)
        k *= 2
    return g


def _inproj_kernel(x_ref, nw_ref, w_ref, lb_ref, qnw_ref, knw_ref, gmat_ref,
                   q_ref, bf_ref, cb_ref, kf_ref, kb_ref, v_ref, sg_ref,
                   nq_ref, nk_ref, nv_ref):
    x = x_ref[0]
    ms = jnp.mean(x * x, axis=-1, keepdims=True)
    h = (x * lax.rsqrt(ms + EPS) * nw_ref[...]).astype(BF16)

    def proj(c):
        return jnp.dot(h, w_ref[:, c * HG_W:(c + 1) * HG_W], preferred_element_type=F32)

    a0 = lb_ref[0]
    a1 = lb_ref[1]
    m = jnp.maximum(a0, a1)
    e0 = jnp.exp(a0 - m)
    e1 = jnp.exp(a1 - m)
    lb = e0 / (e0 + e1)

    q_ref[0] = _silu(proj(0))
    rowmod = lax.broadcasted_iota(jnp.int32, (x.shape[0], HG_W), 0) % CHUNK
    for d, (cum_ref, k_ref) in enumerate(((bf_ref, kf_ref), (cb_ref, kb_ref))):
        lbd = lb[d:d + 1, :]
        f = lbd + (1.0 - lbd) * jax.nn.sigmoid(proj(1 + d))
        k_ref[0] = 1.0 - f
        cum_ref[0] = _chunk_scan(jnp.log(f), rowmod, reverse=(d == 1))
    v_ref[0] = proj(3).astype(BF16)
    sg_ref[0] = _silu(proj(4))

    def head_norm(a, w):
        sq = a * a
        hi = sq.astype(BF16)
        lo = (sq - hi.astype(F32)).astype(BF16)
        ssum = (jnp.dot(hi, gmat_ref[...], preferred_element_type=F32)
                + jnp.dot(lo, gmat_ref[...], preferred_element_type=F32))
        return a * lax.rsqrt(ssum * (1.0 / NA_DH) + EPS) * w

    nq_ref[0] = (head_norm(proj(5), qnw_ref[...]) * (NA_DH ** -0.5)).astype(BF16)
    nk_ref[0] = head_norm(proj(6), knw_ref[...]).astype(BF16)
    nv_ref[0] = proj(7).astype(BF16)


def _inproj(x, norm_w, w_in, lb, qnw, knw, gmat):
    B, T, _ = x.shape
    tm = TOKEN_TILE
    tok = lambda w: pl.BlockSpec((1, tm, w), lambda b, i: (b, i, 0))
    f32_out = jax.ShapeDtypeStruct((B, T, HG_W), F32)
    bf16_out = jax.ShapeDtypeStruct((B, T, HG_W), BF16)
    return pl.pallas_call(
        _inproj_kernel,
        grid=(B, T // tm),
        in_specs=[tok(D_MODEL), _const_spec((1, D_MODEL)), _const_spec(w_in.shape),
                  _const_spec(lb.shape), _const_spec((1, NA_W)), _const_spec((1, NA_W)),
                  _const_spec((NA_W, NA_W))],
        out_specs=[tok(HG_W)] * 10,
        out_shape=[f32_out, f32_out, f32_out, f32_out, f32_out, bf16_out, f32_out,
                   bf16_out, bf16_out, bf16_out],
        compiler_params=pltpu.CompilerParams(
            dimension_semantics=("parallel", "parallel"), vmem_limit_bytes=VMEM_LIMIT_BYTES),
        name="inproj",
    )(x, norm_w, w_in, lb, qnw, knw, gmat)


def _nt_dot(a, b):
    return lax.dot_general(a, b, (((1,), (1,)), ((), ())), preferred_element_type=F32)


def _tn_dot(a, b):
    return lax.dot_general(a, b, (((0,), (0,)), ((), ())), preferred_element_type=F32)


def _hgrn_head(q, kk, cum, v, st, reverse):
    def dif(u, w):
        return (w - u) if reverse else (u - w)

    X, Y = (kk, q) if reverse else (q, kk)
    tot = cum[0:1] if reverse else cum[CHUNK - 1:CHUNK]

    lane = lax.broadcasted_iota(jnp.int32, (8, CHUNK), 1)
    sub = lax.broadcasted_iota(jnp.int32, (8, HG_D), 0)
    blocks = []
    for I in range(CHUNK // SUB):
        lo = I * SUB
        Xb = X[lo:lo + SUB]
        ab = cum[lo:lo + SUB]
        acc = [jnp.zeros((8, CHUNK), F32), jnp.zeros((8, CHUNK), F32)]
        for j in range(SUB):
            yj = Y[lo + j:lo + j + 1]
            aj = cum[lo + j:lo + j + 1]
            for half in range(2):
                if 8 * half + 7 < j:
                    continue
                rows = slice(8 * half, 8 * half + 8)
                p = Xb[rows] * yj * jnp.exp(dif(ab[rows], aj))
                if 8 * half < j:
                    p = jnp.where(sub >= j - 8 * half, p, 0.0)
                col = jnp.sum(p, axis=-1, keepdims=True)
                acc[half] = jnp.where(lane == lo + j, col, acc[half])
        diag = jnp.concatenate(acc, axis=0)
        if I > 0:
            ref = cum[lo - 1:lo]
            xe = (Xb * jnp.exp(dif(ab, ref))).astype(BF16)
            ye = (Y[0:lo] * jnp.exp(dif(ref, cum[0:lo]))).astype(BF16)
            ye = jnp.concatenate([ye, jnp.zeros((CHUNK - lo, HG_D), BF16)], axis=0)
            diag = diag + _nt_dot(xe, ye)
        blocks.append(diag)
    L = jnp.concatenate(blocks, axis=0).astype(BF16)

    st_b = st.astype(BF16)
    qe = (q * jnp.exp(cum)).astype(BF16)
    ke = (kk * jnp.exp(tot - cum)).astype(BF16)
    intra = _tn_dot(L, v) if reverse else jnp.dot(L, v, preferred_element_type=F32)
    out = intra + _nt_dot(qe, st_b)
    st_new = st * jnp.exp(tot) + _tn_dot(v, ke)
    return out, st_new


def _hgrn_kernel(*refs, reverse, tt):
    if reverse:
        q_ref, cum_ref, kk_ref, v_ref, of_ref, sg_ref, gw_ref, o_ref, st_ref = refs
    else:
        q_ref, cum_ref, kk_ref, v_ref, o_ref, st_ref = refs

    @pl.when(pl.program_id(1) == 0)
    def _():
        st_ref[...] = jnp.zeros_like(st_ref)

    nchunk = tt // CHUNK

    def chunk_body(ci, carry):
        c = (nchunk - 1 - ci) if reverse else ci
        rows = pl.ds(pl.multiple_of(c * CHUNK, CHUNK), CHUNK)
        for h in range(HG_HEADS):
            lanes = slice(h * HG_D, (h + 1) * HG_D)
            out, st_new = _hgrn_head(q_ref[0, rows, lanes], kk_ref[0, rows, lanes],
                                     cum_ref[0, rows, lanes], v_ref[0, rows, lanes],
                                     st_ref[h], reverse)
            st_ref[h] = st_new
            if reverse:
                o = out + of_ref[0, rows, lanes]
                ms = jnp.mean(o * o, axis=-1, keepdims=True)
                o = o * lax.rsqrt(ms + EPS) * gw_ref[...] * sg_ref[0, rows, lanes]
                o_ref[0, rows, lanes] = o.astype(o_ref.dtype)
            else:
                o_ref[0, rows, lanes] = out
        return carry

    lax.fori_loop(0, nchunk, chunk_body, 0)


def _hgrn(q, cum, kk, v, reverse, o_fwd=None, sg=None, gnorm_w=None):
    B, T, _ = q.shape
    tt = TOKEN_TILE
    nt = T // tt
    if reverse:
        tile = lambda b, i: (b, nt - 1 - i, 0)
    else:
        tile = lambda b, i: (b, i, 0)
    tok = pl.BlockSpec((1, tt, HG_W), tile)
    args = [q, cum, kk, v]
    in_specs = [tok, tok, tok, tok]
    if reverse:
        args += [o_fwd, sg, gnorm_w]
        in_specs += [tok, tok, _const_spec((1, HG_D))]
    return pl.pallas_call(
        functools.partial(_hgrn_kernel, reverse=reverse, tt=tt),
        grid=(B, nt),
        in_specs=in_specs,
        out_specs=tok,
        out_shape=jax.ShapeDtypeStruct((B, T, HG_W), BF16 if reverse else F32),
        scratch_shapes=[pltpu.VMEM((HG_HEADS, HG_D, HG_D), F32)],
        compiler_params=pltpu.CompilerParams(
            dimension_semantics=("parallel", "arbitrary"), vmem_limit_bytes=VMEM_LIMIT_BYTES),
        name="hgrn_bwd" if reverse else "hgrn_fwd",
    )(*args)


def _na_bias_table(rpb):
    q = np.arange(GRID_W)
    kc = np.arange(GRID_W)
    win_start = np.clip(q - NA_KW // 2, 0, GRID_W - NA_KW)
    valid = (kc[None, :] >= win_start[:, None]) & (kc[None, :] < win_start[:, None] + NA_KW)
    dc = np.clip(kc[None, :] - q[:, None], -(NA_KW - 1), NA_KW - 1) + NA_KW - 1
    e = np.arange(2 * NA_KH - 2)
    half = np.arange(2)
    dr = e[:, None] - (NA_KH - 1) + half[None, :] + NA_KH - 1
    head = 2 * np.arange(NA_HEADS // 2)[:, None] + np.arange(2)[None, :]
    tab = rpb[head[:, None, :, None, None, None],
              dr[None, :, None, None, :, None],
              dc[None, None, None, :, None, :]]
    tab = jnp.where(valid[None, None, None, :, None, :], tab.astype(F32), MASK_VALUE)
    return tab.reshape(NA_HEADS // 2, 2 * NA_KH - 2, 2 * GRID_W, 2 * GRID_W)


def _na_kernel(q_ref, kp_ref, kc_ref, kn_ref, vp_ref, vc_ref, vn_ref, bias_ref, o_ref,
               kbuf, vbuf, *, rows):
    i = pl.program_id(1)
    tile_tokens = NA_TILE_ROWS * GRID_W
    for n, (kr, vr) in enumerate(((kp_ref, vp_ref), (kc_ref, vc_ref), (kn_ref, vn_ref))):
        kbuf[n * tile_tokens:(n + 1) * tile_tokens] = kr[0]
        vbuf[n * tile_tokens:(n + 1) * tile_tokens] = vr[0]

    lane = lax.broadcasted_iota(jnp.int32, (GRID_W, 2 * NA_DH), 1)
    first = lane < NA_DH

    def row_body(j, carry):
        r = i * NA_TILE_ROWS + j
        start = jnp.clip(r - NA_KH // 2, 0, rows - NA_KH)
        shift = r - start
        koff = pl.multiple_of((start - i * NA_TILE_ROWS + NA_TILE_ROWS) * GRID_W, GRID_W)
        qrows = pl.ds(pl.multiple_of(j * GRID_W, GRID_W), GRID_W)
        for p in range(NA_HEADS // 2):
            lanes = slice(p * 2 * NA_DH, (p + 1) * 2 * NA_DH)
            qp = q_ref[0, qrows, lanes]
            zero = jnp.zeros_like(qp)
            qab = jnp.concatenate([jnp.where(first, qp, zero), jnp.where(first, zero, qp)], axis=0)
            kwin = kbuf[pl.ds(koff, NA_KH * GRID_W), lanes]
            vwin = vbuf[pl.ds(koff, NA_KH * GRID_W), lanes]
            s = _nt_dot(qab, kwin)
            bias = jnp.concatenate(
                [bias_ref[p, NA_KH - 1 - shift + 2 * m] for m in range(NA_KH // 2)], axis=1)
            s = s + bias
            mx = jnp.max(s, axis=-1, keepdims=True)
            e = jnp.exp(s - mx)
            den = jnp.sum(e, axis=-1, keepdims=True)
            pv = jnp.dot(e.astype(BF16), vwin, preferred_element_type=F32) / den
            o_ref[0, qrows, lanes] = jnp.where(first, pv[:GRID_W], pv[GRID_W:]).astype(o_ref.dtype)
        return carry

    lax.fori_loop(0, NA_TILE_ROWS, row_body, 0)


def _na(nq, nk, nv, bias):
    B, T, _ = nq.shape
    rows = T // GRID_W
    tile_tokens = NA_TILE_ROWS * GRID_W
    nt = rows // NA_TILE_ROWS
    cur = pl.BlockSpec((1, tile_tokens, NA_W), lambda b, i: (b, i, 0))
    prev = pl.BlockSpec((1, tile_tokens, NA_W), lambda b, i: (b, jnp.maximum(i - 1, 0), 0))
    nxt = pl.BlockSpec((1, tile_tokens, NA_W), lambda b, i: (b, jnp.minimum(i + 1, nt - 1), 0))
    return pl.pallas_call(
        functools.partial(_na_kernel, rows=rows),
        grid=(B, nt),
        in_specs=[cur, prev, cur, nxt, prev, cur, nxt, _const_spec(bias.shape)],
        out_specs=cur,
        out_shape=jax.ShapeDtypeStruct((B, T, NA_W), BF16),
        scratch_shapes=[pltpu.VMEM((3 * tile_tokens, NA_W), BF16),
                        pltpu.VMEM((3 * tile_tokens, NA_W), BF16)],
        compiler_params=pltpu.CompilerParams(
            dimension_semantics=("parallel", "parallel"), vmem_limit_bytes=VMEM_LIMIT_BYTES),
        name="natten",
    )(nq, nk, nk, nk, nv, nv, nv, bias)


def _ffn_kernel(x_ref, oh_ref, on_ref, wo_ref, nw_ref, wg_ref, wu_ref, wd_ref, y_ref):
    x = x_ref[0]
    x = x + jnp.dot(oh_ref[0], wo_ref[:HG_W], preferred_element_type=F32)
    x = x + jnp.dot(on_ref[0], wo_ref[HG_W:], preferred_element_type=F32)
    ms = jnp.mean(x * x, axis=-1, keepdims=True)
    h = (x * lax.rsqrt(ms + EPS) * nw_ref[...]).astype(BF16)
    gate = jnp.dot(h, wg_ref[...], preferred_element_type=F32)
    up = jnp.dot(h, wu_ref[...], preferred_element_type=F32)
    act = (_silu(gate) * up).astype(BF16)
    y_ref[0] = x + jnp.dot(act, wd_ref[...], preferred_element_type=F32)


def _ffn(x, o_hgrn, o_na, w_out, norm_w, w_gate, w_up, w_down):
    B, T, _ = x.shape
    tm = TOKEN_TILE
    tok = lambda w: pl.BlockSpec((1, tm, w), lambda b, i: (b, i, 0))
    return pl.pallas_call(
        _ffn_kernel,
        grid=(B, T // tm),
        in_specs=[tok(D_MODEL), tok(HG_W), tok(NA_W), _const_spec(w_out.shape),
                  _const_spec((1, D_MODEL)), _const_spec(w_gate.shape),
                  _const_spec(w_up.shape), _const_spec(w_down.shape)],
        out_specs=tok(D_MODEL),
        out_shape=jax.ShapeDtypeStruct(x.shape, x.dtype),
        compiler_params=pltpu.CompilerParams(
            dimension_semantics=("parallel", "parallel"), vmem_limit_bytes=VMEM_LIMIT_BYTES),
        name="outproj_ffn",
    )(x, o_hgrn, o_na, w_out, norm_w, w_gate, w_up, w_down)


def _encoder_layer(x, p):
    q, bf, cb, kf, kb, v, sg, nq, nk, nv = _inproj(
        x, p["norm_mix_w"], p["w_in"], p["lb"], p["qnw"], p["knw"], p["gmat"])
    o_fwd = _hgrn(q, bf, kf, v, reverse=False)
    o_hgrn = _hgrn(q, cb, kb, v, reverse=True, o_fwd=o_fwd, sg=sg, gnorm_w=p["gnorm_w"])
    o_na = _na(nq, nk, nv, p["na_bias"])
    return _ffn(x, o_hgrn, o_na, p["w_out"], p["norm_ffn_w"], p["w_gate"], p["w_up"], p["w_down"])


def kernel(x_prompt, x_sample, norm_mix_w, w_in, hgrn_lb, hgrn_gnorm_w, na_q_norm_w, na_k_norm_w,
           na_rpb, w_out, norm_ffn_w, w_gate, w_up, w_down):
    head_of_lane = np.arange(NA_W) // NA_DH
    p = {
        "norm_mix_w": norm_mix_w[0][None].astype(F32),
        "w_in": w_in[0].astype(BF16),
        "lb": hgrn_lb.astype(F32),
        "qnw": jnp.tile(na_q_norm_w[0].astype(F32), NA_HEADS)[None],
        "knw": jnp.tile(na_k_norm_w[0].astype(F32), NA_HEADS)[None],
        "gmat": jnp.asarray(head_of_lane[:, None] == head_of_lane[None, :], dtype=BF16),
        "gnorm_w": hgrn_gnorm_w[0][None].astype(F32),
        "na_bias": _na_bias_table(na_rpb[0]),
        "w_out": w_out[0].astype(BF16),
        "norm_ffn_w": norm_ffn_w[0][None].astype(F32),
        "w_gate": w_gate[0].astype(BF16),
        "w_up": w_up[0].astype(BF16),
        "w_down": w_down[0].astype(BF16),
    }
    return (_encoder_layer(x_prompt, p), _encoder_layer(x_sample, p))
```

```python
import functools

import numpy as np
import jax
import jax.numpy as jnp
from jax import lax
from jax.experimental import pallas as pl
from jax.experimental.pallas import tpu as pltpu

F32 = jnp.float32
BF16 = jnp.bfloat16

D_MODEL = 1024
GRID_W = 64
HG_HEADS = 4
HG_D = 128
HG_W = HG_HEADS * HG_D
CHUNK = 64
SUB = 16
NA_HEADS = 8
NA_DH = 64
NA_W = NA_HEADS * NA_DH
NA_KH = 8
NA_KW = 16
D_FF = 2816
EPS = 1e-6
MASK_VALUE = -1e30

TOKEN_TILE = 512
NA_TILE_ROWS = 8
VMEM_LIMIT_BYTES = 56 * 1024 * 1024


def _silu(x):
    return x * jax.nn.sigmoid(x)


def _const_spec(shape):
    return pl.BlockSpec(shape, lambda *_: (0,) * len(shape), pipeline_mode=pl.Buffered(1))


def _chunk_scan(g, rowmod, reverse):
    n = g.shape[0]
    k = 1
    while k < CHUNK:
        if reverse:
            shifted = pltpu.roll(g, n - k, 0)
            g = g + jnp.where(rowmod < CHUNK - k, shifted, 0.0)
        else:
            shifted = pltpu.roll(g, k, 0)
            g = g + jnp.where(rowmod >= k, shifted, 0.0)
        k *= 2
    return g


def _inproj_kernel(x_ref, nw_ref, w_ref, lb_ref, qnw_ref, knw_ref, gmat_ref,
                   q_ref, bf_ref, cb_ref, kf_ref, kb_ref, v_ref, sg_ref,
                   nq_ref, nk_ref, nv_ref):
    x = x_ref[0]
    ms = jnp.mean(x * x, axis=-1, keepdims=True)
    h = (x * lax.rsqrt(ms + EPS) * nw_ref[...]).astype(BF16)

    def proj(c):
        return jnp.dot(h, w_ref[:, c * HG_W:(c + 1) * HG_W], preferred_element_type=F32)

    a0 = lb_ref[0]
    a1 = lb_ref[1]
    m = jnp.maximum(a0, a1)
    e0 = jnp.exp(a0 - m)
    e1 = jnp.exp(a1 - m)
    lb = e0 / (e0 + e1)

    q_ref[0] = _silu(proj(0))
    rowmod = lax.broadcasted_iota(jnp.int32, (x.shape[0], HG_W), 0) % CHUNK
    for d, (cum_ref, k_ref) in enumerate(((bf_ref, kf_ref), (cb_ref, kb_ref))):
        lbd = lb[d:d + 1, :]
        f = lbd + (1.0 - lbd) * jax.nn.sigmoid(proj(1 + d))
        k_ref[0] = 1.0 - f
        cum_ref[0] = _chunk_scan(jnp.log(f), rowmod, reverse=(d == 1))
    v_ref[0] = proj(3).astype(BF16)
    sg_ref[0] = _silu(proj(4))

    def head_norm(a, w):
        sq = a * a
        hi = sq.astype(BF16)
        lo = (sq - hi.astype(F32)).astype(BF16)
        ssum = (jnp.dot(hi, gmat_ref[...], preferred_element_type=F32)
                + jnp.dot(lo, gmat_ref[...], preferred_element_type=F32))
        return a * lax.rsqrt(ssum * (1.0 / NA_DH) + EPS) * w

    nq_ref[0] = (head_norm(proj(5), qnw_ref[...]) * (NA_DH ** -0.5)).astype(BF16)
    nk_ref[0] = head_norm(proj(6), knw_ref[...]).astype(BF16)
    nv_ref[0] = proj(7).astype(BF16)


def _inproj(x, norm_w, w_in, lb, qnw, knw, gmat):
    B, T, _ = x.shape
    tm = TOKEN_TILE
    tok = lambda w: pl.BlockSpec((1, tm, w), lambda b, i: (b, i, 0))
    f32_out = jax.ShapeDtypeStruct((B, T, HG_W), F32)
    bf16_out = jax.ShapeDtypeStruct((B, T, HG_W), BF16)
    return pl.pallas_call(
        _inproj_kernel,
        grid=(B, T // tm),
        in_specs=[tok(D_MODEL), _const_spec((1, D_MODEL)), _const_spec(w_in.shape),
                  _const_spec(lb.shape), _const_spec((1, NA_W)), _const_spec((1, NA_W)),
                  _const_spec((NA_W, NA_W))],
        out_specs=[tok(HG_W)] * 10,
        out_shape=[f32_out, f32_out, f32_out, f32_out, f32_out, bf16_out, f32_out,
                   bf16_out, bf16_out, bf16_out],
        compiler_params=pltpu.CompilerParams(
            dimension_semantics=("parallel", "parallel"), vmem_limit_bytes=VMEM_LIMIT_BYTES),
        name="inproj",
    )(x, norm_w, w_in, lb, qnw, knw, gmat)


def _nt_dot(a, b):
    return lax.dot_general(a, b, (((1,), (1,)), ((), ())), preferred_element_type=F32)


def _tn_dot(a, b):
    return lax.dot_general(a, b, (((0,), (0,)), ((), ())), preferred_element_type=F32)


def _hgrn_head(q, kk, cum, v, st, reverse):
    def dif(u, w):
        return (w - u) if reverse else (u - w)

    X, Y = (kk, q) if reverse else (q, kk)
    tot = cum[0:1] if reverse else cum[CHUNK - 1:CHUNK]

    lane = lax.broadcasted_iota(jnp.int32, (8, CHUNK), 1)
    sub = lax.broadcasted_iota(jnp.int32, (8, HG_D), 0)
    blocks = []
    for I in range(CHUNK // SUB):
        lo = I * SUB
        Xb = X[lo:lo + SUB]
        ab = cum[lo:lo + SUB]
        acc = [jnp.zeros((8, CHUNK), F32), jnp.zeros((8, CHUNK), F32)]
        for j in range(SUB):
            yj = Y[lo + j:lo + j + 1]
            aj = cum[lo + j:lo + j + 1]
            for half in range(2):
                if 8 * half + 7 < j:
                    continue
                rows = slice(8 * half, 8 * half + 8)
                p = Xb[rows] * yj * jnp.exp(dif(ab[rows], aj))
                if 8 * half < j:
                    p = jnp.where(sub >= j - 8 * half, p, 0.0)
                col = jnp.sum(p, axis=-1, keepdims=True)
                acc[half] = jnp.where(lane == lo + j, col, acc[half])
        diag = jnp.concatenate(acc, axis=0)
        if I > 0:
            ref = cum[lo - 1:lo]
            xe = (Xb * jnp.exp(dif(ab, ref))).astype(BF16)
            ye = (Y[0:lo] * jnp.exp(dif(ref, cum[0:lo]))).astype(BF16)
            ye = jnp.concatenate([ye, jnp.zeros((CHUNK - lo, HG_D), BF16)], axis=0)
            diag = diag + _nt_dot(xe, ye)
        blocks.append(diag)
    L = jnp.concatenate(blocks, axis=0).astype(BF16)

    st_b = st.astype(BF16)
    qe = (q * jnp.exp(cum)).astype(BF16)
    ke = (kk * jnp.exp(tot - cum)).astype(BF16)
    intra = _tn_dot(L, v) if reverse else jnp.dot(L, v, preferred_element_type=F32)
    out = intra + _nt_dot(qe, st_b)
    st_new = st * jnp.exp(tot) + _tn_dot(v, ke)
    return out, st_new


def _hgrn_kernel(*refs, reverse, tt):
    if reverse:
        q_ref, cum_ref, kk_ref, v_ref, of_ref, sg_ref, gw_ref, o_ref, st_ref = refs
    else:
        q_ref, cum_ref, kk_ref, v_ref, o_ref, st_ref = refs

    @pl.when(pl.program_id(1) == 0)
    def _():
        st_ref[...] = jnp.zeros_like(st_ref)

    nchunk = tt // CHUNK

    def chunk_body(ci, carry):
        c = (nchunk - 1 - ci) if reverse else ci
        rows = pl.ds(pl.multiple_of(c * CHUNK, CHUNK), CHUNK)
        for h in range(HG_HEADS):
            lanes = slice(h * HG_D, (h + 1) * HG_D)
            out, st_new = _hgrn_head(q_ref[0, rows, lanes], kk_ref[0, rows, lanes],
                                     cum_ref[0, rows, lanes], v_ref[0, rows, lanes],
                                     st_ref[h], reverse)
            st_ref[h] = st_new
            if reverse:
                o = out + of_ref[0, rows, lanes]
                ms = jnp.mean(o * o, axis=-1, keepdims=True)
                o = o * lax.rsqrt(ms + EPS) * gw_ref[...] * sg_ref[0, rows, lanes]
                o_ref[0, rows, lanes] = o.astype(o_ref.dtype)
            else:
                o_ref[0, rows, lanes] = out
        return carry

    lax.fori_loop(0, nchunk, chunk_body, 0)


def _hgrn(q, cum, kk, v, reverse, o_fwd=None, sg=None, gnorm_w=None):
    B, T, _ = q.shape
    tt = TOKEN_TILE
    nt = T // tt
    if reverse:
        tile = lambda b, i: (b, nt - 1 - i, 0)
    else:
        tile = lambda b, i: (b, i, 0)
    tok = pl.BlockSpec((1, tt, HG_W), tile)
    args = [q, cum, kk, v]
    in_specs = [tok, tok, tok, tok]
    if reverse:
        args += [o_fwd, sg, gnorm_w]
        in_specs += [tok, tok, _const_spec((1, HG_D))]
    return pl.pallas_call(
        functools.partial(_hgrn_kernel, reverse=reverse, tt=tt),
        grid=(B, nt),
        in_specs=in_specs,
        out_specs=tok,
        out_shape=jax.ShapeDtypeStruct((B, T, HG_W), BF16 if reverse else F32),
        scratch_shapes=[pltpu.VMEM((HG_HEADS, HG_D, HG_D), F32)],
        compiler_params=pltpu.CompilerParams(
            dimension_semantics=("parallel", "arbitrary"), vmem_limit_bytes=VMEM_LIMIT_BYTES),
        name="hgrn_bwd" if reverse else "hgrn_fwd",
    )(*args)


def _na_bias_table(rpb):
    q = np.arange(GRID_W)
    kc = np.arange(GRID_W)
    win_start = np.clip(q - NA_KW // 2, 0, GRID_W - NA_KW)
    valid = (kc[None, :] >= win_start[:, None]) & (kc[None, :] < win_start[:, None] + NA_KW)
    dc = np.clip(kc[None, :] - q[:, None], -(NA_KW - 1), NA_KW - 1) + NA_KW - 1
    onehot = jnp.asarray(dc[:, :, None] == np.arange(2 * NA_KW - 1), dtype=F32)
    toep = jnp.einsum("hrc,qkc->hrqk", rpb.astype(F32), onehot, precision=lax.Precision.HIGHEST)
    toep = jnp.where(valid[None, None], toep, MASK_VALUE)
    n_e = 2 * NA_KH - 2
    tab = jnp.stack([toep[:, 0:n_e], toep[:, 1:n_e + 1]], axis=3)
    tab = tab.reshape(NA_HEADS // 2, 2, n_e, GRID_W, 2, GRID_W)
    tab = tab.transpose(0, 2, 1, 3, 4, 5)
    return tab.reshape(NA_HEADS // 2, n_e, 2 * GRID_W, 2 * GRID_W)


def _na_kernel(q_ref, kp_ref, kc_ref, kn_ref, vp_ref, vc_ref, vn_ref, bias_ref, o_ref,
               kbuf, vbuf, *, rows):
    i = pl.program_id(1)
    tile_tokens = NA_TILE_ROWS * GRID_W
    for n, (kr, vr) in enumerate(((kp_ref, vp_ref), (kc_ref, vc_ref), (kn_ref, vn_ref))):
        kbuf[n * tile_tokens:(n + 1) * tile_tokens] = kr[0]
        vbuf[n * tile_tokens:(n + 1) * tile_tokens] = vr[0]

    lane = lax.broadcasted_iota(jnp.int32, (GRID_W, 2 * NA_DH), 1)
    first = lane < NA_DH

    def row_body(j, carry):
        r = i * NA_TILE_ROWS + j
        start = jnp.clip(r - NA_KH // 2, 0, rows - NA_KH)
        shift = r - start
        koff = pl.multiple_of((start - i * NA_TILE_ROWS + NA_TILE_ROWS) * GRID_W, GRID_W)
        qrows = pl.ds(pl.multiple_of(j * GRID_W, GRID_W), GRID_W)
        for p in range(NA_HEADS // 2):
            lanes = slice(p * 2 * NA_DH, (p + 1) * 2 * NA_DH)
            qp = q_ref[0, qrows, lanes]
            zero = jnp.zeros_like(qp)
            qab = jnp.concatenate([jnp.where(first, qp, zero), jnp.where(first, zero, qp)], axis=0)
            kwin = kbuf[pl.ds(koff, NA_KH * GRID_W), lanes]
            vwin = vbuf[pl.ds(koff, NA_KH * GRID_W), lanes]
            s = _nt_dot(qab, kwin)
            bias = jnp.concatenate(
                [bias_ref[p, NA_KH - 1 - shift + 2 * m] for m in range(NA_KH // 2)], axis=1)
            s = s + bias
            mx = jnp.max(s, axis=-1, keepdims=True)
            e = jnp.exp(s - mx)
            den = jnp.sum(e, axis=-1, keepdims=True)
            pv = jnp.dot(e.astype(BF16), vwin, preferred_element_type=F32) / den
            o_ref[0, qrows, lanes] = jnp.where(first, pv[:GRID_W], pv[GRID_W:]).astype(o_ref.dtype)
        return carry

    lax.fori_loop(0, NA_TILE_ROWS, row_body, 0)


def _na(nq, nk, nv, bias):
    B, T, _ = nq.shape
    rows = T // GRID_W
    tile_tokens = NA_TILE_ROWS * GRID_W
    nt = rows // NA_TILE_ROWS
    cur = pl.BlockSpec((1, tile_tokens, NA_W), lambda b, i: (b, i, 0))
    prev = pl.BlockSpec((1, tile_tokens, NA_W), lambda b, i: (b, jnp.maximum(i - 1, 0), 0))
    nxt = pl.BlockSpec((1, tile_tokens, NA_W), lambda b, i: (b, jnp.minimum(i + 1, nt - 1), 0))
    return pl.pallas_call(
        functools.partial(_na_kernel, rows=rows),
        grid=(B, nt),
        in_specs=[cur, prev, cur, nxt, prev, cur, nxt, _const_spec(bias.shape)],
        out_specs=cur,
        out_shape=jax.ShapeDtypeStruct((B, T, NA_W), BF16),
        scratch_shapes=[pltpu.VMEM((3 * tile_tokens, NA_W), BF16),
                        pltpu.VMEM((3 * tile_tokens, NA_W), BF16)],
        compiler_params=pltpu.CompilerParams(
            dimension_semantics=("parallel", "parallel"), vmem_limit_bytes=VMEM_LIMIT_BYTES),
        name="natten",
    )(nq, nk, nk, nk, nv, nv, nv, bias)


def _ffn_kernel(x_ref, oh_ref, on_ref, wo_ref, nw_ref, wg_ref, wu_ref, wd_ref, y_ref):
    x = x_ref[0]
    x = x + jnp.dot(oh_ref[0], wo_ref[:HG_W], preferred_element_type=F32)
    x = x + jnp.dot(on_ref[0], wo_ref[HG_W:], preferred_element_type=F32)
    ms = jnp.mean(x * x, axis=-1, keepdims=True)
    h = (x * lax.rsqrt(ms + EPS) * nw_ref[...]).astype(BF16)
    gate = jnp.dot(h, wg_ref[...], preferred_element_type=F32)
    up = jnp.dot(h, wu_ref[...], preferred_element_type=F32)
    act = (_silu(gate) * up).astype(BF16)
    y_ref[0] = x + jnp.dot(act, wd_ref[...], preferred_element_type=F32)


def _ffn(x, o_hgrn, o_na, w_out, norm_w, w_gate, w_up, w_down):
    B, T, _ = x.shape
    tm = TOKEN_TILE
    tok = lambda w: pl.BlockSpec((1, tm, w), lambda b, i: (b, i, 0))
    return pl.pallas_call(
        _ffn_kernel,
        grid=(B, T // tm),
        in_specs=[tok(D_MODEL), tok(HG_W), tok(NA_W), _const_spec(w_out.shape),
                  _const_spec((1, D_MODEL)), _const_spec(w_gate.shape),
                  _const_spec(w_up.shape), _const_spec(w_down.shape)],
        out_specs=tok(D_MODEL),
        out_shape=jax.ShapeDtypeStruct(x.shape, x.dtype),
        compiler_params=pltpu.CompilerParams(
            dimension_semantics=("parallel", "parallel"), vmem_limit_bytes=VMEM_LIMIT_BYTES),
        name="outproj_ffn",
    )(x, o_hgrn, o_na, w_out, norm_w, w_gate, w_up, w_down)


def _encoder_layer(x, p):
    q, bf, cb, kf, kb, v, sg, nq, nk, nv = _inproj(
        x, p["norm_mix_w"], p["w_in"], p["lb"], p["qnw"], p["knw"], p["gmat"])
    o_fwd = _hgrn(q, bf, kf, v, reverse=False)
    o_hgrn = _hgrn(q, cb, kb, v, reverse=True, o_fwd=o_fwd, sg=sg, gnorm_w=p["gnorm_w"])
    o_na = _na(nq, nk, nv, p["na_bias"])
    return _ffn(x, o_hgrn, o_na, p["w_out"], p["norm_ffn_w"], p["w_gate"], p["w_up"], p["w_down"])


def kernel(x_prompt, x_sample, norm_mix_w, w_in, hgrn_lb, hgrn_gnorm_w, na_q_norm_w, na_k_norm_w,
           na_rpb, w_out, norm_ffn_w, w_gate, w_up, w_down):
    head_of_lane = np.arange(NA_W) // NA_DH
    p = {
        "norm_mix_w": norm_mix_w[0][None].astype(F32),
        "w_in": w_in[0].astype(BF16),
        "lb": hgrn_lb.astype(F32),
        "qnw": jnp.tile(na_q_norm_w[0].astype(F32), NA_HEADS)[None],
        "knw": jnp.tile(na_k_norm_w[0].astype(F32), NA_HEADS)[None],
        "gmat": jnp.asarray(head_of_lane[:, None] == head_of_lane[None, :], dtype=BF16),
        "gnorm_w": hgrn_gnorm_w[0][None].astype(F32),
        "na_bias": _na_bias_table(na_rpb[0]),
        "w_out": w_out[0].astype(BF16),
        "norm_ffn_w": norm_ffn_w[0][None].astype(F32),
        "w_gate": w_gate[0].astype(BF16),
        "w_up": w_up[0].astype(BF16),
        "w_down": w_down[0].astype(BF16),
    }
    return (_encoder_layer(x_prompt, p), _encoder_layer(x_sample, p))
```

```python
import functools

import numpy as np
import jax
import jax.numpy as jnp
from jax import lax
from jax.experimental import pallas as pl
from jax.experimental.pallas import tpu as pltpu

F32 = jnp.float32
BF16 = jnp.bfloat16

D_MODEL = 1024
GRID_W = 64
HG_HEADS = 4
HG_D = 128
HG_W = HG_HEADS * HG_D
CHUNK = 64
SUB = 8
NA_HEADS = 8
NA_DH = 64
NA_W = NA_HEADS * NA_DH
NA_KH = 8
NA_KW = 16
D_FF = 2816
EPS = 1e-6
MASK_VALUE = -1e30

TOKEN_TILE = 512
NA_TILE_ROWS = 8
VMEM_LIMIT_BYTES = 56 * 1024 * 1024


def _silu(x):
    return x * jax.nn.sigmoid(x)


def _const_spec(shape):
    return pl.BlockSpec(shape, lambda *_: (0,) * len(shape), pipeline_mode=pl.Buffered(1))


def _chunk_scan(g, rowmod, reverse):
    n = g.shape[0]
    k = 1
    while k < CHUNK:
        if reverse:
            shifted = pltpu.roll(g, n - k, 0)
            g = g + jnp.where(rowmod < CHUNK - k, shifted, 0.0)
        else:
            shifted = pltpu.roll(g, k, 0)
            g = g + jnp.where(rowmod >= k, shifted, 0.0)
        k *= 2
    return g


def _inproj_kernel(x_ref, nw_ref, w_ref, lb_ref, qnw_ref, knw_ref, gmat_ref,
                   q_ref, bf_ref, cb_ref, kf_ref, kb_ref, v_ref, sg_ref,
                   nq_ref, nk_ref, nv_ref):
    x = x_ref[0]
    ms = jnp.mean(x * x, axis=-1, keepdims=True)
    h = (x * lax.rsqrt(ms + EPS) * nw_ref[...]).astype(BF16)

    def proj(c):
        return jnp.dot(h, w_ref[:, c * HG_W:(c + 1) * HG_W], preferred_element_type=F32)

    a0 = lb_ref[0]
    a1 = lb_ref[1]
    m = jnp.maximum(a0, a1)
    e0 = jnp.exp(a0 - m)
    e1 = jnp.exp(a1 - m)
    lb = e0 / (e0 + e1)

    q_ref[0] = _silu(proj(0))
    rowmod = lax.broadcasted_iota(jnp.int32, (x.shape[0], HG_W), 0) % CHUNK
    for d, (cum_ref, k_ref) in enumerate(((bf_ref, kf_ref), (cb_ref, kb_ref))):
        lbd = lb[d:d + 1, :]
        f = lbd + (1.0 - lbd) * jax.nn.sigmoid(proj(1 + d))
        k_ref[0] = 1.0 - f
        cum_ref[0] = _chunk_scan(jnp.log2(f), rowmod, reverse=(d == 1))
    v_ref[0] = proj(3).astype(BF16)
    sg_ref[0] = _silu(proj(4))

    def head_norm(a, w):
        sq = a * a
        hi = sq.astype(BF16)
        lo = (sq - hi.astype(F32)).astype(BF16)
        ssum = (jnp.dot(hi, gmat_ref[...], preferred_element_type=F32)
                + jnp.dot(lo, gmat_ref[...], preferred_element_type=F32))
        return a * lax.rsqrt(ssum * (1.0 / NA_DH) + EPS) * w

    nq_ref[0] = (head_norm(proj(5), qnw_ref[...]) * (NA_DH ** -0.5)).astype(BF16)
    nk_ref[0] = head_norm(proj(6), knw_ref[...]).astype(BF16)
    nv_ref[0] = proj(7).astype(BF16)


def _inproj(x, norm_w, w_in, lb, qnw, knw, gmat):
    B, T, _ = x.shape
    tm = TOKEN_TILE
    tok = lambda w: pl.BlockSpec((1, tm, w), lambda b, i: (b, i, 0))
    f32_out = jax.ShapeDtypeStruct((B, T, HG_W), F32)
    bf16_out = jax.ShapeDtypeStruct((B, T, HG_W), BF16)
    return pl.pallas_call(
        _inproj_kernel,
        grid=(B, T // tm),
        in_specs=[tok(D_MODEL), _const_spec((1, D_MODEL)), _const_spec(w_in.shape),
                  _const_spec(lb.shape), _const_spec((1, NA_W)), _const_spec((1, NA_W)),
                  _const_spec((NA_W, NA_W))],
        out_specs=[tok(HG_W)] * 10,
        out_shape=[f32_out, f32_out, f32_out, f32_out, f32_out, bf16_out, f32_out,
                   bf16_out, bf16_out, bf16_out],
        compiler_params=pltpu.CompilerParams(
            dimension_semantics=("parallel", "parallel"), vmem_limit_bytes=VMEM_LIMIT_BYTES),
        name="inproj",
    )(x, norm_w, w_in, lb, qnw, knw, gmat)


def _nt_dot(a, b):
    return lax.dot_general(a, b, (((1,), (1,)), ((), ())), preferred_element_type=F32)


def _tn_dot(a, b):
    return lax.dot_general(a, b, (((0,), (0,)), ((), ())), preferred_element_type=F32)


def _hgrn_intra(q_ref, kk_ref, cum_ref, v_ref, c, lanes, reverse):
    def ld(ref, lo, n):
        return ref[0, c, lo:lo + n, lanes]

    def dif(u, w):
        return (w - u) if reverse else (u - w)

    x_ref, y_ref = (kk_ref, q_ref) if reverse else (q_ref, kk_ref)
    q = ld(q_ref, 0, CHUNK)
    kk = ld(kk_ref, 0, CHUNK)
    cum = ld(cum_ref, 0, CHUNK)
    v = ld(v_ref, 0, CHUNK)
    X, Y = (kk, q) if reverse else (q, kk)
    tot = ld(cum_ref, 0 if reverse else CHUNK - 1, 1)
    nblk = CHUNK // SUB

    lane = lax.broadcasted_iota(jnp.int32, (SUB, CHUNK), 1)
    sub = lax.broadcasted_iota(jnp.int32, (SUB, CHUNK), 0)
    blocks = []
    for I in range(nblk):
        lo = I * SUB
        Xb = X[lo:lo + SUB]
        ab = cum[lo:lo + SUB]
        blk = jnp.zeros((SUB, CHUNK), F32)
        for j in range(SUB):
            yj = ld(y_ref, lo + j, 1)
            aj = ld(cum_ref, lo + j, 1)
            col = jnp.sum(Xb * yj * jnp.exp2(dif(ab, aj)), axis=-1, keepdims=True)
            blk = jnp.where(lane == lo + j, col, blk)
        blocks.append(jnp.where(lane - lo <= sub, blk, 0.0))

    ends = [ld(cum_ref, J * SUB + SUB - 1, 1) for J in range(nblk - 1)]
    ye = [Y[J * SUB:(J + 1) * SUB] * jnp.exp2(dif(ends[J], cum[J * SUB:(J + 1) * SUB]))
          for J in range(nblk - 1)]
    ye = jnp.concatenate(ye + [jnp.zeros((SUB, HG_D), F32)], axis=0).astype(BF16)
    xe = [X[(J + 1) * SUB:] * jnp.exp2(dif(cum[(J + 1) * SUB:], ends[J])) for J in range(nblk - 1)]
    res = _nt_dot(jnp.concatenate(xe, axis=0).astype(BF16), ye)
    lane_blk = lane // SUB
    off = 0
    for J in range(nblk - 1):
        for I in range(J + 1, nblk):
            r0 = off + (I - J - 1) * SUB
            blocks[I] = jnp.where(lane_blk == J, res[r0:r0 + SUB], blocks[I])
        off += CHUNK - (J + 1) * SUB
    L = jnp.concatenate(blocks, axis=0).astype(BF16)

    intra = _tn_dot(L, v) if reverse else jnp.dot(L, v, preferred_element_type=F32)
    qe = (q * jnp.exp2(cum)).astype(BF16)
    ke = (kk * jnp.exp2(tot - cum)).astype(BF16)
    return intra, _tn_dot(v, ke), qe, jnp.exp2(tot)


def _hgrn_kernel(*refs, reverse, tt):
    if reverse:
        q_ref, cum_ref, kk_ref, v_ref, of_ref, sg_ref, gw_ref, o_ref = refs[:8]
    else:
        q_ref, cum_ref, kk_ref, v_ref, o_ref = refs[:5]
    st_ref = refs[-9]
    slots = (refs[-8:-4], refs[-4:])

    @pl.when(pl.program_id(1) == 0)
    def _():
        st_ref[...] = jnp.zeros_like(st_ref)

    nchunk = tt // CHUNK

    def chunk_of(ci):
        return (nchunk - 1 - ci) if reverse else ci

    def intra(ci, slot):
        o_buf, u_buf, q_buf, d_buf = slot
        for h in range(HG_HEADS):
            lanes = slice(h * HG_D, (h + 1) * HG_D)
            o, u, qe, dec = _hgrn_intra(q_ref, kk_ref, cum_ref, v_ref, chunk_of(ci), lanes, reverse)
            o_buf[h] = o
            u_buf[h] = u
            q_buf[h] = qe
            d_buf[h] = jnp.broadcast_to(dec, d_buf.shape[1:])

    def inter(ci, slot):
        o_buf, u_buf, q_buf, d_buf = slot
        c = chunk_of(ci)
        for h in range(HG_HEADS):
            lanes = slice(h * HG_D, (h + 1) * HG_D)
            st = st_ref[h]
            out = o_buf[h] + _nt_dot(q_buf[h], st.astype(BF16))
            st_ref[h] = st * d_buf[h][0:1] + u_buf[h]
            if reverse:
                o = out + of_ref[0, c, :, lanes]
                ms = jnp.mean(o * o, axis=-1, keepdims=True)
                o = o * lax.rsqrt(ms + EPS) * gw_ref[...] * sg_ref[0, c, :, lanes]
                o_ref[0, c, :, lanes] = o.astype(o_ref.dtype)
            else:
                o_ref[0, c, :, lanes] = out

    intra(0, slots[0])

    def two_chunks(t, carry):
        ci = 2 * t
        intra(ci + 1, slots[1])
        inter(ci, slots[0])
        intra(ci + 2, slots[0])
        inter(ci + 1, slots[1])
        return carry

    lax.fori_loop(0, nchunk // 2 - 1, two_chunks, 0)
    intra(nchunk - 1, slots[1])
    inter(nchunk - 2, slots[0])
    inter(nchunk - 1, slots[1])


def _hgrn(q, cum, kk, v, reverse, o_fwd=None, sg=None, gnorm_w=None):
    B, T, _ = q.shape
    tt = TOKEN_TILE
    nt = T // tt
    if reverse:
        tile = lambda b, i: (b, nt - 1 - i, 0, 0)
    else:
        tile = lambda b, i: (b, i, 0, 0)
    chunked = lambda a: a.reshape(B, T // CHUNK, CHUNK, HG_W)
    tok = pl.BlockSpec((1, tt // CHUNK, CHUNK, HG_W), tile)
    args = [chunked(a) for a in (q, cum, kk, v)]
    in_specs = [tok, tok, tok, tok]
    if reverse:
        args += [chunked(o_fwd), chunked(sg), gnorm_w]
        in_specs += [tok, tok, _const_spec((1, HG_D))]
    out = pl.pallas_call(
        functools.partial(_hgrn_kernel, reverse=reverse, tt=tt),
        grid=(B, nt),
        in_specs=in_specs,
        out_specs=tok,
        out_shape=jax.ShapeDtypeStruct((B, T // CHUNK, CHUNK, HG_W), BF16 if reverse else F32),
        scratch_shapes=[pltpu.VMEM((HG_HEADS, HG_D, HG_D), F32)] + 2 * [
            pltpu.VMEM((HG_HEADS, CHUNK, HG_D), F32), pltpu.VMEM((HG_HEADS, HG_D, HG_D), F32),
            pltpu.VMEM((HG_HEADS, CHUNK, HG_D), BF16), pltpu.VMEM((HG_HEADS, 8, HG_D), F32)],
        compiler_params=pltpu.CompilerParams(
            dimension_semantics=("parallel", "arbitrary"), vmem_limit_bytes=VMEM_LIMIT_BYTES),
        name="hgrn_bwd" if reverse else "hgrn_fwd",
    )(*args)
    return out.reshape(B, T, HG_W)


def _na_bias_table(rpb):
    q = np.arange(GRID_W)
    kc = np.arange(GRID_W)
    win_start = np.clip(q - NA_KW // 2, 0, GRID_W - NA_KW)
    valid = (kc[None, :] >= win_start[:, None]) & (kc[None, :] < win_start[:, None] + NA_KW)
    dc = np.clip(kc[None, :] - q[:, None], -(NA_KW - 1), NA_KW - 1) + NA_KW - 1
    onehot = jnp.asarray(dc[:, :, None] == np.arange(2 * NA_KW - 1), dtype=F32)
    toep = jnp.einsum("hrc,qkc->hrqk", rpb.astype(F32), onehot, precision=lax.Precision.HIGHEST)
    toep = jnp.where(valid[None, None], toep, MASK_VALUE)
    n_e = 2 * NA_KH - 2
    tab = jnp.stack([toep[:, 0:n_e], toep[:, 1:n_e + 1]], axis=3)
    tab = tab.reshape(NA_HEADS // 2, 2, n_e, GRID_W, 2, GRID_W)
    tab = tab.transpose(0, 2, 1, 3, 4, 5)
    return tab.reshape(NA_HEADS // 2, n_e, 2 * GRID_W, 2 * GRID_W)


def _na_kernel(q_ref, kp_ref, kc_ref, kn_ref, vp_ref, vc_ref, vn_ref, bias_ref, o_ref,
               kbuf, vbuf, sbuf0, sbuf1, *, rows):
    i = pl.program_id(1)
    tile_tokens = NA_TILE_ROWS * GRID_W
    for n, (kr, vr) in enumerate(((kp_ref, vp_ref), (kc_ref, vc_ref), (kn_ref, vn_ref))):
        kbuf[n * tile_tokens:(n + 1) * tile_tokens] = kr[0]
        vbuf[n * tile_tokens:(n + 1) * tile_tokens] = vr[0]

    lane = lax.broadcasted_iota(jnp.int32, (GRID_W, 2 * NA_DH), 1)
    first = lane < NA_DH
    npairs = NA_HEADS // 2

    def window(j):
        r = i * NA_TILE_ROWS + j
        start = jnp.clip(r - NA_KH // 2, 0, rows - NA_KH)
        koff = pl.multiple_of((start - i * NA_TILE_ROWS + NA_TILE_ROWS) * GRID_W, GRID_W)
        return r - start, pl.ds(koff, NA_KH * GRID_W), pl.ds(pl.multiple_of(j * GRID_W, GRID_W), GRID_W)

    def scores(j, sbuf):
        shift, krows, qrows = window(j)
        for p in range(npairs):
            lanes = slice(p * 2 * NA_DH, (p + 1) * 2 * NA_DH)
            qp = q_ref[0, qrows, lanes]
            zero = jnp.zeros_like(qp)
            qab = jnp.concatenate([jnp.where(first, qp, zero), jnp.where(first, zero, qp)], axis=0)
            s = _nt_dot(qab, kbuf[krows, lanes])
            for m in range(NA_KH // 2):
                cols = slice(m * 2 * GRID_W, (m + 1) * 2 * GRID_W)
                sbuf[p, :, cols] = s[:, cols] + bias_ref[p, NA_KH - 1 - shift + 2 * m]

    def attend(j, sbuf):
        _, krows, qrows = window(j)
        for p in range(npairs):
            lanes = slice(p * 2 * NA_DH, (p + 1) * 2 * NA_DH)
            s = sbuf[p]
            mx = jnp.max(s, axis=-1, keepdims=True)
            e = jnp.exp(s - mx)
            den = jnp.sum(e, axis=-1, keepdims=True)
            pv = jnp.dot(e.astype(BF16), vbuf[krows, lanes], preferred_element_type=F32) / den
            o_ref[0, qrows, lanes] = jnp.where(first, pv[:GRID_W], pv[GRID_W:]).astype(o_ref.dtype)

    scores(0, sbuf0)

    def two_rows(t, carry):
        j = 2 * t
        scores(j + 1, sbuf1)
        attend(j, sbuf0)
        scores(j + 2, sbuf0)
        attend(j + 1, sbuf1)
        return carry

    lax.fori_loop(0, NA_TILE_ROWS // 2 - 1, two_rows, 0)
    scores(NA_TILE_ROWS - 1, sbuf1)
    attend(NA_TILE_ROWS - 2, sbuf0)
    attend(NA_TILE_ROWS - 1, sbuf1)


def _na(nq, nk, nv, bias):
    B, T, _ = nq.shape
    rows = T // GRID_W
    tile_tokens = NA_TILE_ROWS * GRID_W
    nt = rows // NA_TILE_ROWS
    cur = pl.BlockSpec((1, tile_tokens, NA_W), lambda b, i: (b, i, 0))
    prev = pl.BlockSpec((1, tile_tokens, NA_W), lambda b, i: (b, jnp.maximum(i - 1, 0), 0))
    nxt = pl.BlockSpec((1, tile_tokens, NA_W), lambda b, i: (b, jnp.minimum(i + 1, nt - 1), 0))
    return pl.pallas_call(
        functools.partial(_na_kernel, rows=rows),
        grid=(B, nt),
        in_specs=[cur, prev, cur, nxt, prev, cur, nxt, _const_spec(bias.shape)],
        out_specs=cur,
        out_shape=jax.ShapeDtypeStruct((B, T, NA_W), BF16),
        scratch_shapes=[pltpu.VMEM((3 * tile_tokens, NA_W), BF16),
                        pltpu.VMEM((3 * tile_tokens, NA_W), BF16),
                        pltpu.VMEM((NA_HEADS // 2, 2 * GRID_W, NA_KH * GRID_W), F32),
                        pltpu.VMEM((NA_HEADS // 2, 2 * GRID_W, NA_KH * GRID_W), F32)],
        compiler_params=pltpu.CompilerParams(
            dimension_semantics=("parallel", "parallel"), vmem_limit_bytes=VMEM_LIMIT_BYTES),
        name="natten",
    )(nq, nk, nk, nk, nv, nv, nv, bias)


def _ffn_kernel(x_ref, oh_ref, on_ref, wo_ref, nw_ref, wg_ref, wu_ref, wd_ref, y_ref):
    x = x_ref[0]
    x = x + jnp.dot(oh_ref[0], wo_ref[:HG_W], preferred_element_type=F32)
    x = x + jnp.dot(on_ref[0], wo_ref[HG_W:], preferred_element_type=F32)
    ms = jnp.mean(x * x, axis=-1, keepdims=True)
    h = (x * lax.rsqrt(ms + EPS) * nw_ref[...]).astype(BF16)
    gate = jnp.dot(h, wg_ref[...], preferred_element_type=F32)
    up = jnp.dot(h, wu_ref[...], preferred_element_type=F32)
    act = (_silu(gate) * up).astype(BF16)
    y_ref[0] = x + jnp.dot(act, wd_ref[...], preferred_element_type=F32)


def _ffn(x, o_hgrn, o_na, w_out, norm_w, w_gate, w_up, w_down):
    B, T, _ = x.shape
    tm = TOKEN_TILE
    tok = lambda w: pl.BlockSpec((1, tm, w), lambda b, i: (b, i, 0))
    return pl.pallas_call(
        _ffn_kernel,
        grid=(B, T // tm),
        in_specs=[tok(D_MODEL), tok(HG_W), tok(NA_W), _const_spec(w_out.shape),
                  _const_spec((1, D_MODEL)), _const_spec(w_gate.shape),
                  _const_spec(w_up.shape), _const_spec(w_down.shape)],
        out_specs=tok(D_MODEL),
        out_shape=jax.ShapeDtypeStruct(x.shape, x.dtype),
        compiler_params=pltpu.CompilerParams(
            dimension_semantics=("parallel", "parallel"), vmem_limit_bytes=VMEM_LIMIT_BYTES),
        name="outproj_ffn",
    )(x, o_hgrn, o_na, w_out, norm_w, w_gate, w_up, w_down)


def _encoder_layer(x, p):
    q, bf, cb, kf, kb, v, sg, nq, nk, nv = _inproj(
        x, p["norm_mix_w"], p["w_in"], p["lb"], p["qnw"], p["knw"], p["gmat"])
    o_fwd = _hgrn(q, bf, kf, v, reverse=False)
    o_hgrn = _hgrn(q, cb, kb, v, reverse=True, o_fwd=o_fwd, sg=sg, gnorm_w=p["gnorm_w"])
    o_na = _na(nq, nk, nv, p["na_bias"])
    return _ffn(x, o_hgrn, o_na, p["w_out"], p["norm_ffn_w"], p["w_gate"], p["w_up"], p["w_down"])


def kernel(x_prompt, x_sample, norm_mix_w, w_in, hgrn_lb, hgrn_gnorm_w, na_q_norm_w, na_k_norm_w,
           na_rpb, w_out, norm_ffn_w, w_gate, w_up, w_down):
    head_of_lane = np.arange(NA_W) // NA_DH
    p = {
        "norm_mix_w": norm_mix_w[0][None].astype(F32),
        "w_in": w_in[0].astype(BF16),
        "lb": hgrn_lb.astype(F32),
        "qnw": jnp.tile(na_q_norm_w[0].astype(F32), NA_HEADS)[None],
        "knw": jnp.tile(na_k_norm_w[0].astype(F32), NA_HEADS)[None],
        "gmat": jnp.asarray(head_of_lane[:, None] == head_of_lane[None, :], dtype=BF16),
        "gnorm_w": hgrn_gnorm_w[0][None].astype(F32),
        "na_bias": _na_bias_table(na_rpb[0]),
        "w_out": w_out[0].astype(BF16),
        "norm_ffn_w": norm_ffn_w[0][None].astype(F32),
        "w_gate": w_gate[0].astype(BF16),
        "w_up": w_up[0].astype(BF16),
        "w_down": w_down[0].astype(BF16),
    }
    return (_encoder_layer(x_prompt, p), _encoder_layer(x_sample, p))
```

```python
import functools

import numpy as np
import jax
import jax.numpy as jnp
from jax import lax
from jax.experimental import pallas as pl
from jax.experimental.pallas import tpu as pltpu

F32 = jnp.float32
BF16 = jnp.bfloat16

D_MODEL = 1024
GRID_W = 64
HG_HEADS = 4
HG_D = 128
HG_W = HG_HEADS * HG_D
CHUNK = 64
SUB = 8
NA_HEADS = 8
NA_DH = 64
NA_W = NA_HEADS * NA_DH
NA_KH = 8
NA_KW = 16
D_FF = 2816
EPS = 1e-6
MASK_VALUE = -1e30

TOKEN_TILE = 512
NA_TILE_ROWS = 8
VMEM_LIMIT_BYTES = 56 * 1024 * 1024


def _silu(x):
    return x * jax.nn.sigmoid(x)


def _const_spec(shape):
    return pl.BlockSpec(shape, lambda *_: (0,) * len(shape), pipeline_mode=pl.Buffered(1))


def _chunk_scan(g, rowmod, reverse):
    n = g.shape[0]
    k = 1
    while k < CHUNK:
        if reverse:
            shifted = pltpu.roll(g, n - k, 0)
            g = g + jnp.where(rowmod < CHUNK - k, shifted, 0.0)
        else:
            shifted = pltpu.roll(g, k, 0)
            g = g + jnp.where(rowmod >= k, shifted, 0.0)
        k *= 2
    return g


def _inproj_kernel(x_ref, nw_ref, w_ref, lb_ref, qnw_ref, knw_ref, gmat_ref,
                   q_ref, bf_ref, cb_ref, kf_ref, kb_ref, v_ref, sg_ref,
                   nq_ref, nk_ref, nv_ref):
    x = x_ref[0]
    ms = jnp.mean(x * x, axis=-1, keepdims=True)
    h = (x * lax.rsqrt(ms + EPS) * nw_ref[...]).astype(BF16)

    def proj(c):
        return jnp.dot(h, w_ref[:, c * HG_W:(c + 1) * HG_W], preferred_element_type=F32)

    a0 = lb_ref[0]
    a1 = lb_ref[1]
    m = jnp.maximum(a0, a1)
    e0 = jnp.exp(a0 - m)
    e1 = jnp.exp(a1 - m)
    lb = e0 / (e0 + e1)

    rowmod = lax.broadcasted_iota(jnp.int32, (x.shape[0], HG_W), 0) % CHUNK

    def gate(a, d, cum_ref, k_ref):
        lbd = lb[d:d + 1, :]
        f = lbd + (1.0 - lbd) * jax.nn.sigmoid(a)
        k_ref[0] = 1.0 - f
        cum_ref[0] = _chunk_scan(jnp.log2(f), rowmod, reverse=(d == 1))

    def head_norm(a, w):
        ssum = jnp.dot((a * a).astype(BF16), gmat_ref[...], preferred_element_type=F32)
        return a * lax.rsqrt(ssum * (1.0 / NA_DH) + EPS) * w

    def store(ref, val):
        ref[0] = val.astype(ref.dtype)

    post = [
        lambda a: store(q_ref, _silu(a)),
        lambda a: gate(a, 0, bf_ref, kf_ref),
        lambda a: gate(a, 1, cb_ref, kb_ref),
        lambda a: store(v_ref, a),
        lambda a: store(sg_ref, _silu(a)),
        lambda a: store(nq_ref, head_norm(a, qnw_ref[...]) * (NA_DH ** -0.5)),
        lambda a: store(nk_ref, head_norm(a, knw_ref[...])),
        lambda a: store(nv_ref, a),
    ]
    ahead = 2
    pending = [proj(c) for c in range(ahead)]
    for c in range(len(post)):
        if c + ahead < len(post):
            pending.append(proj(c + ahead))
        post[c](pending.pop(0))


def _inproj(x, norm_w, w_in, lb, qnw, knw, gmat):
    B, T, _ = x.shape
    tm = TOKEN_TILE
    tok = lambda w: pl.BlockSpec((1, tm, w), lambda b, i: (b, i, 0))
    f32_out = jax.ShapeDtypeStruct((B, T, HG_W), F32)
    bf16_out = jax.ShapeDtypeStruct((B, T, HG_W), BF16)
    return pl.pallas_call(
        _inproj_kernel,
        grid=(B, T // tm),
        in_specs=[tok(D_MODEL), _const_spec((1, D_MODEL)), _const_spec(w_in.shape),
                  _const_spec(lb.shape), _const_spec((1, NA_W)), _const_spec((1, NA_W)),
                  _const_spec((NA_W, NA_W))],
        out_specs=[tok(HG_W)] * 10,
        out_shape=[f32_out, f32_out, f32_out, f32_out, f32_out, bf16_out, f32_out,
                   bf16_out, bf16_out, bf16_out],
        compiler_params=pltpu.CompilerParams(
            dimension_semantics=("parallel", "parallel"), vmem_limit_bytes=VMEM_LIMIT_BYTES),
        name="inproj",
    )(x, norm_w, w_in, lb, qnw, knw, gmat)


def _nt_dot(a, b):
    return lax.dot_general(a, b, (((1,), (1,)), ((), ())), preferred_element_type=F32)


def _tn_dot(a, b):
    return lax.dot_general(a, b, (((0,), (0,)), ((), ())), preferred_element_type=F32)


def _hgrn_intra(q_ref, kk_ref, cum_ref, v_ref, c, lanes, reverse):
    def ld(ref, lo, n):
        return ref[0, c, lo:lo + n, lanes]

    def dif(u, w):
        return (w - u) if reverse else (u - w)

    x_ref, y_ref = (kk_ref, q_ref) if reverse else (q_ref, kk_ref)
    q = ld(q_ref, 0, CHUNK)
    kk = ld(kk_ref, 0, CHUNK)
    cum = ld(cum_ref, 0, CHUNK)
    v = ld(v_ref, 0, CHUNK)
    X, Y = (kk, q) if reverse else (q, kk)
    tot = ld(cum_ref, 0 if reverse else CHUNK - 1, 1)
    nblk = CHUNK // SUB

    lane = lax.broadcasted_iota(jnp.int32, (SUB, CHUNK), 1)
    sub = lax.broadcasted_iota(jnp.int32, (SUB, CHUNK), 0)
    blocks = []
    for I in range(nblk):
        lo = I * SUB
        Xb = X[lo:lo + SUB]
        ab = cum[lo:lo + SUB]
        blk = jnp.zeros((SUB, CHUNK), F32)
        for j in range(SUB):
            yj = ld(y_ref, lo + j, 1)
            aj = ld(cum_ref, lo + j, 1)
            col = jnp.sum(Xb * yj * jnp.exp2(dif(ab, aj)), axis=-1, keepdims=True)
            blk = jnp.where(lane == lo + j, col, blk)
        blocks.append(jnp.where(lane - lo <= sub, blk, 0.0))

    ends = [ld(cum_ref, J * SUB + SUB - 1, 1) for J in range(nblk - 1)]
    ye = [Y[J * SUB:(J + 1) * SUB] * jnp.exp2(dif(ends[J], cum[J * SUB:(J + 1) * SUB]))
          for J in range(nblk - 1)]
    ye = jnp.concatenate(ye + [jnp.zeros((SUB, HG_D), F32)], axis=0).astype(BF16)
    xe = [X[(J + 1) * SUB:] * jnp.exp2(dif(cum[(J + 1) * SUB:], ends[J])) for J in range(nblk - 1)]
    res = _nt_dot(jnp.concatenate(xe, axis=0).astype(BF16), ye)
    lane_blk = lane // SUB
    off = 0
    for J in range(nblk - 1):
        for I in range(J + 1, nblk):
            r0 = off + (I - J - 1) * SUB
            blocks[I] = jnp.where(lane_blk == J, res[r0:r0 + SUB], blocks[I])
        off += CHUNK - (J + 1) * SUB
    L = jnp.concatenate(blocks, axis=0).astype(BF16)

    intra = _tn_dot(L, v) if reverse else jnp.dot(L, v, preferred_element_type=F32)
    qe = (q * jnp.exp2(cum)).astype(BF16)
    ke = (kk * jnp.exp2(tot - cum)).astype(BF16)
    return intra, _tn_dot(v, ke), qe, jnp.exp2(tot)


def _hgrn_kernel(*refs, reverse, tt):
    if reverse:
        q_ref, cum_ref, kk_ref, v_ref, of_ref, sg_ref, gw_ref, o_ref = refs[:8]
    else:
        q_ref, cum_ref, kk_ref, v_ref, o_ref = refs[:5]
    st_ref = refs[-9]
    slots = (refs[-8:-4], refs[-4:])

    @pl.when(pl.program_id(1) == 0)
    def _():
        st_ref[...] = jnp.zeros_like(st_ref)

    nchunk = tt // CHUNK

    def chunk_of(ci):
        return (nchunk - 1 - ci) if reverse else ci

    def intra(ci, slot):
        o_buf, u_buf, q_buf, d_buf = slot
        for h in range(HG_HEADS):
            lanes = slice(h * HG_D, (h + 1) * HG_D)
            o, u, qe, dec = _hgrn_intra(q_ref, kk_ref, cum_ref, v_ref, chunk_of(ci), lanes, reverse)
            o_buf[h] = o
            u_buf[h] = u
            q_buf[h] = qe
            d_buf[h] = jnp.broadcast_to(dec, d_buf.shape[1:])

    def inter(ci, slot):
        o_buf, u_buf, q_buf, d_buf = slot
        c = chunk_of(ci)
        for h in range(HG_HEADS):
            lanes = slice(h * HG_D, (h + 1) * HG_D)
            st = st_ref[h]
            out = o_buf[h] + _nt_dot(q_buf[h], st.astype(BF16))
            st_ref[h] = st * d_buf[h][0:1] + u_buf[h]
            if reverse:
                o = out + of_ref[0, c, :, lanes]
                ms = jnp.mean(o * o, axis=-1, keepdims=True)
                o = o * lax.rsqrt(ms + EPS) * gw_ref[...] * sg_ref[0, c, :, lanes]
                o_ref[0, c, :, lanes] = o.astype(o_ref.dtype)
            else:
                o_ref[0, c, :, lanes] = out

    intra(0, slots[0])
    for ci in range(nchunk):
        if ci + 1 < nchunk:
            intra(ci + 1, slots[(ci + 1) % 2])
        inter(ci, slots[ci % 2])


def _hgrn(q, cum, kk, v, reverse, o_fwd=None, sg=None, gnorm_w=None):
    B, T, _ = q.shape
    tt = TOKEN_TILE
    nt = T // tt
    if reverse:
        tile = lambda b, i: (b, nt - 1 - i, 0, 0)
    else:
        tile = lambda b, i: (b, i, 0, 0)
    chunked = lambda a: a.reshape(B, T // CHUNK, CHUNK, HG_W)
    tok = pl.BlockSpec((1, tt // CHUNK, CHUNK, HG_W), tile)
    args = [chunked(a) for a in (q, cum, kk, v)]
    in_specs = [tok, tok, tok, tok]
    if reverse:
        args += [chunked(o_fwd), chunked(sg), gnorm_w]
        in_specs += [tok, tok, _const_spec((1, HG_D))]
    out = pl.pallas_call(
        functools.partial(_hgrn_kernel, reverse=reverse, tt=tt),
        grid=(B, nt),
        in_specs=in_specs,
        out_specs=tok,
        out_shape=jax.ShapeDtypeStruct((B, T // CHUNK, CHUNK, HG_W), BF16 if reverse else F32),
        scratch_shapes=[pltpu.VMEM((HG_HEADS, HG_D, HG_D), F32)] + 2 * [
            pltpu.VMEM((HG_HEADS, CHUNK, HG_D), F32), pltpu.VMEM((HG_HEADS, HG_D, HG_D), F32),
            pltpu.VMEM((HG_HEADS, CHUNK, HG_D), BF16), pltpu.VMEM((HG_HEADS, 8, HG_D), F32)],
        compiler_params=pltpu.CompilerParams(
            dimension_semantics=("parallel", "arbitrary"), vmem_limit_bytes=VMEM_LIMIT_BYTES),
        name="hgrn_bwd" if reverse else "hgrn_fwd",
    )(*args)
    return out.reshape(B, T, HG_W)


def _na_bias_table(rpb):
    q = np.arange(GRID_W)
    kc = np.arange(GRID_W)
    win_start = np.clip(q - NA_KW // 2, 0, GRID_W - NA_KW)
    valid = (kc[None, :] >= win_start[:, None]) & (kc[None, :] < win_start[:, None] + NA_KW)
    dc = np.clip(kc[None, :] - q[:, None], -(NA_KW - 1), NA_KW - 1) + NA_KW - 1
    onehot = jnp.asarray(dc[:, :, None] == np.arange(2 * NA_KW - 1), dtype=F32)
    toep = jnp.einsum("hrc,qkc->hrqk", rpb.astype(F32), onehot, precision=lax.Precision.HIGHEST)
    toep = jnp.where(valid[None, None], toep, MASK_VALUE)
    n_e = 2 * NA_KH - 2
    tab = jnp.stack([toep[:, 0:n_e], toep[:, 1:n_e + 1]], axis=3)
    tab = tab.reshape(NA_HEADS // 2, 2, n_e, GRID_W, 2, GRID_W)
    tab = tab.transpose(0, 2, 1, 3, 4, 5)
    return tab.reshape(NA_HEADS // 2, n_e, 2 * GRID_W, 2 * GRID_W)


def _na_kernel(q_ref, kp_ref, kc_ref, kn_ref, vp_ref, vc_ref, vn_ref, bias_ref, o_ref,
               kbuf, vbuf, sbuf0, sbuf1, *, rows):
    i = pl.program_id(1)
    tile_tokens = NA_TILE_ROWS * GRID_W
    for n, (kr, vr) in enumerate(((kp_ref, vp_ref), (kc_ref, vc_ref), (kn_ref, vn_ref))):
        kbuf[n * tile_tokens:(n + 1) * tile_tokens] = kr[0]
        vbuf[n * tile_tokens:(n + 1) * tile_tokens] = vr[0]

    lane = lax.broadcasted_iota(jnp.int32, (GRID_W, 2 * NA_DH), 1)
    first = lane < NA_DH
    npairs = NA_HEADS // 2

    def window(j):
        r = i * NA_TILE_ROWS + j
        start = jnp.clip(r - NA_KH // 2, 0, rows - NA_KH)
        koff = pl.multiple_of((start - i * NA_TILE_ROWS + NA_TILE_ROWS) * GRID_W, GRID_W)
        return r - start, pl.ds(koff, NA_KH * GRID_W), pl.ds(pl.multiple_of(j * GRID_W, GRID_W), GRID_W)

    def scores(j, sbuf):
        shift, krows, qrows = window(j)
        for p in range(npairs):
            lanes = slice(p * 2 * NA_DH, (p + 1) * 2 * NA_DH)
            qp = q_ref[0, qrows, lanes]
            zero = jnp.zeros_like(qp)
            qab = jnp.concatenate([jnp.where(first, qp, zero), jnp.where(first, zero, qp)], axis=0)
            s = _nt_dot(qab, kbuf[krows, lanes])
            for m in range(NA_KH // 2):
                cols = slice(m * 2 * GRID_W, (m + 1) * 2 * GRID_W)
                sbuf[p, :, cols] = s[:, cols] + bias_ref[p, NA_KH - 1 - shift + 2 * m]

    def attend(j, sbuf):
        _, krows, qrows = window(j)
        for p in range(npairs):
            lanes = slice(p * 2 * NA_DH, (p + 1) * 2 * NA_DH)
            s = sbuf[p]
            mx = jnp.max(s, axis=-1, keepdims=True)
            e = jnp.exp(s - mx)
            den = jnp.sum(e, axis=-1, keepdims=True)
            pv = jnp.dot(e.astype(BF16), vbuf[krows, lanes], preferred_element_type=F32) / den
            o_ref[0, qrows, lanes] = jnp.where(first, pv[:GRID_W], pv[GRID_W:]).astype(o_ref.dtype)

    sbufs = (sbuf0, sbuf1)
    scores(0, sbufs[0])
    for j in range(NA_TILE_ROWS):
        if j + 1 < NA_TILE_ROWS:
            scores(j + 1, sbufs[(j + 1) % 2])
        attend(j, sbufs[j % 2])


def _na(nq, nk, nv, bias):
    B, T, _ = nq.shape
    rows = T // GRID_W
    tile_tokens = NA_TILE_ROWS * GRID_W
    nt = rows // NA_TILE_ROWS
    cur = pl.BlockSpec((1, tile_tokens, NA_W), lambda b, i: (b, i, 0))
    prev = pl.BlockSpec((1, tile_tokens, NA_W), lambda b, i: (b, jnp.maximum(i - 1, 0), 0))
    nxt = pl.BlockSpec((1, tile_tokens, NA_W), lambda b, i: (b, jnp.minimum(i + 1, nt - 1), 0))
    return pl.pallas_call(
        functools.partial(_na_kernel, rows=rows),
        grid=(B, nt),
        in_specs=[cur, prev, cur, nxt, prev, cur, nxt, _const_spec(bias.shape)],
        out_specs=cur,
        out_shape=jax.ShapeDtypeStruct((B, T, NA_W), BF16),
        scratch_shapes=[pltpu.VMEM((3 * tile_tokens, NA_W), BF16),
                        pltpu.VMEM((3 * tile_tokens, NA_W), BF16),
                        pltpu.VMEM((NA_HEADS // 2, 2 * GRID_W, NA_KH * GRID_W), F32),
                        pltpu.VMEM((NA_HEADS // 2, 2 * GRID_W, NA_KH * GRID_W), F32)],
        compiler_params=pltpu.CompilerParams(
            dimension_semantics=("parallel", "parallel"), vmem_limit_bytes=VMEM_LIMIT_BYTES),
        name="natten",
    )(nq, nk, nk, nk, nv, nv, nv, bias)


def _ffn_kernel(x_ref, oh_ref, on_ref, wo_ref, nw_ref, wg_ref, wu_ref, wd_ref, y_ref):
    x = x_ref[0]
    x = x + jnp.dot(oh_ref[0], wo_ref[:HG_W], preferred_element_type=F32)
    x = x + jnp.dot(on_ref[0], wo_ref[HG_W:], preferred_element_type=F32)
    ms = jnp.mean(x * x, axis=-1, keepdims=True)
    h = (x * lax.rsqrt(ms + EPS) * nw_ref[...]).astype(BF16)
    gate = jnp.dot(h, wg_ref[...], preferred_element_type=F32)
    up = jnp.dot(h, wu_ref[...], preferred_element_type=F32)
    act = (_silu(gate) * up).astype(BF16)
    y_ref[0] = x + jnp.dot(act, wd_ref[...], preferred_element_type=F32)


def _ffn(x, o_hgrn, o_na, w_out, norm_w, w_gate, w_up, w_down):
    B, T, _ = x.shape
    tm = TOKEN_TILE
    tok = lambda w: pl.BlockSpec((1, tm, w), lambda b, i: (b, i, 0))
    return pl.pallas_call(
        _ffn_kernel,
        grid=(B, T // tm),
        in_specs=[tok(D_MODEL), tok(HG_W), tok(NA_W), _const_spec(w_out.shape),
                  _const_spec((1, D_MODEL)), _const_spec(w_gate.shape),
                  _const_spec(w_up.shape), _const_spec(w_down.shape)],
        out_specs=tok(D_MODEL),
        out_shape=jax.ShapeDtypeStruct(x.shape, x.dtype),
        compiler_params=pltpu.CompilerParams(
            dimension_semantics=("parallel", "parallel"), vmem_limit_bytes=VMEM_LIMIT_BYTES),
        name="outproj_ffn",
    )(x, o_hgrn, o_na, w_out, norm_w, w_gate, w_up, w_down)


def _encoder_layer(x, p):
    q, bf, cb, kf, kb, v, sg, nq, nk, nv = _inproj(
        x, p["norm_mix_w"], p["w_in"], p["lb"], p["qnw"], p["knw"], p["gmat"])
    o_fwd = _hgrn(q, bf, kf, v, reverse=False)
    o_hgrn = _hgrn(q, cb, kb, v, reverse=True, o_fwd=o_fwd, sg=sg, gnorm_w=p["gnorm_w"])
    o_na = _na(nq, nk, nv, p["na_bias"])
    return _ffn(x, o_hgrn, o_na, p["w_out"], p["norm_ffn_w"], p["w_gate"], p["w_up"], p["w_down"])


def kernel(x_prompt, x_sample, norm_mix_w, w_in, hgrn_lb, hgrn_gnorm_w, na_q_norm_w, na_k_norm_w,
           na_rpb, w_out, norm_ffn_w, w_gate, w_up, w_down):
    head_of_lane = np.arange(NA_W) // NA_DH
    p = {
        "norm_mix_w": norm_mix_w[0][None].astype(F32),
        "w_in": w_in[0].astype(BF16),
        "lb": hgrn_lb.astype(F32),
        "qnw": jnp.tile(na_q_norm_w[0].astype(F32), NA_HEADS)[None],
        "knw": jnp.tile(na_k_norm_w[0].astype(F32), NA_HEADS)[None],
        "gmat": jnp.asarray(head_of_lane[:, None] == head_of_lane[None, :], dtype=BF16),
        "gnorm_w": hgrn_gnorm_w[0][None].astype(F32),
        "na_bias": _na_bias_table(na_rpb[0]),
        "w_out": w_out[0].astype(BF16),
        "norm_ffn_w": norm_ffn_w[0][None].astype(F32),
        "w_gate": w_gate[0].astype(BF16),
        "w_up": w_up[0].astype(BF16),
        "w_down": w_down[0].astype(BF16),
    }
    return (_encoder_layer(x_prompt, p), _encoder_layer(x_sample, p))
```

```python
import functools

import numpy as np
import jax
import jax.numpy as jnp
from jax import lax
from jax.experimental import pallas as pl
from jax.experimental.pallas import tpu as pltpu

F32 = jnp.float32
BF16 = jnp.bfloat16

D_MODEL = 1024
GRID_W = 64
HG_HEADS = 4
HG_D = 128
HG_W = HG_HEADS * HG_D
CHUNK = 64
SUB = 8
FAST_SUB = 16
MAX_BLOCK_DECAY_LOG2 = 80.0
NA_HEADS = 8
NA_DH = 64
NA_W = NA_HEADS * NA_DH
NA_KH = 8
NA_KW = 16
D_FF = 2816
EPS = 1e-6
MASK_VALUE = -1e30

TOKEN_TILE = 512
NA_TILE_ROWS = 8
VMEM_LIMIT_BYTES = 56 * 1024 * 1024


def _silu(x):
    return x * jax.nn.sigmoid(x)


def _const_spec(shape):
    return pl.BlockSpec(shape, lambda *_: (0,) * len(shape), pipeline_mode=pl.Buffered(1))


def _chunk_scan(g, rowmod, reverse):
    n = g.shape[0]
    k = 1
    while k < CHUNK:
        if reverse:
            shifted = pltpu.roll(g, n - k, 0)
            g = g + jnp.where(rowmod < CHUNK - k, shifted, 0.0)
        else:
            shifted = pltpu.roll(g, k, 0)
            g = g + jnp.where(rowmod >= k, shifted, 0.0)
        k *= 2
    return g


def _inproj_kernel(x_ref, nw_ref, w_ref, lb_ref, qnw_ref, knw_ref, gmat_ref,
                   q_ref, bf_ref, cb_ref, kf_ref, kb_ref, v_ref, sg_ref,
                   nq_ref, nk_ref, nv_ref):
    x = x_ref[0]
    ms = jnp.mean(x * x, axis=-1, keepdims=True)
    h = (x * lax.rsqrt(ms + EPS) * nw_ref[...]).astype(BF16)

    def proj(c):
        return jnp.dot(h, w_ref[:, c * HG_W:(c + 1) * HG_W], preferred_element_type=F32)

    a0 = lb_ref[0]
    a1 = lb_ref[1]
    m = jnp.maximum(a0, a1)
    e0 = jnp.exp(a0 - m)
    e1 = jnp.exp(a1 - m)
    lb = e0 / (e0 + e1)

    rowmod = lax.broadcasted_iota(jnp.int32, (x.shape[0], HG_W), 0) % CHUNK

    def gate(a, d, cum_ref, k_ref):
        lbd = lb[d:d + 1, :]
        f = lbd + (1.0 - lbd) * jax.nn.sigmoid(a)
        k_ref[0] = 1.0 - f
        cum_ref[0] = _chunk_scan(jnp.log2(f), rowmod, reverse=(d == 1))

    def head_norm(a, w):
        ssum = jnp.dot((a * a).astype(BF16), gmat_ref[...], preferred_element_type=F32)
        return a * lax.rsqrt(ssum * (1.0 / NA_DH) + EPS) * w

    def store(ref, val):
        ref[0] = val.astype(ref.dtype)

    post = [
        lambda a: store(q_ref, _silu(a)),
        lambda a: gate(a, 0, bf_ref, kf_ref),
        lambda a: gate(a, 1, cb_ref, kb_ref),
        lambda a: store(v_ref, a),
        lambda a: store(sg_ref, _silu(a)),
        lambda a: store(nq_ref, head_norm(a, qnw_ref[...]) * (NA_DH ** -0.5)),
        lambda a: store(nk_ref, head_norm(a, knw_ref[...])),
        lambda a: store(nv_ref, a),
    ]
    ahead = 2
    pending = [proj(c) for c in range(ahead)]
    for c in range(len(post)):
        if c + ahead < len(post):
            pending.append(proj(c + ahead))
        post[c](pending.pop(0))


def _inproj(x, norm_w, w_in, lb, qnw, knw, gmat):
    B, T, _ = x.shape
    tm = TOKEN_TILE
    tok = lambda w: pl.BlockSpec((1, tm, w), lambda b, i: (b, i, 0))
    f32_out = jax.ShapeDtypeStruct((B, T, HG_W), F32)
    bf16_out = jax.ShapeDtypeStruct((B, T, HG_W), BF16)
    return pl.pallas_call(
        _inproj_kernel,
        grid=(B, T // tm),
        in_specs=[tok(D_MODEL), _const_spec((1, D_MODEL)), _const_spec(w_in.shape),
                  _const_spec(lb.shape), _const_spec((1, NA_W)), _const_spec((1, NA_W)),
                  _const_spec((NA_W, NA_W))],
        out_specs=[tok(HG_W)] * 10,
        out_shape=[f32_out, f32_out, f32_out, f32_out, f32_out, bf16_out, f32_out,
                   bf16_out, bf16_out, bf16_out],
        compiler_params=pltpu.CompilerParams(
            dimension_semantics=("parallel", "parallel"), vmem_limit_bytes=VMEM_LIMIT_BYTES),
        name="inproj",
    )(x, norm_w, w_in, lb, qnw, knw, gmat)


def _nt_dot(a, b):
    return lax.dot_general(a, b, (((1,), (1,)), ((), ())), preferred_element_type=F32)


def _tn_dot(a, b):
    return lax.dot_general(a, b, (((0,), (0,)), ((), ())), preferred_element_type=F32)


def _hgrn_intra(q_ref, kk_ref, cum_ref, v_ref, c, lanes, reverse, exact):
    def ld(ref, lo, n):
        return ref[0, c, lo:lo + n, lanes]

    def dif(u, w):
        return (w - u) if reverse else (u - w)

    x_ref, y_ref = (kk_ref, q_ref) if reverse else (q_ref, kk_ref)
    q = ld(q_ref, 0, CHUNK)
    kk = ld(kk_ref, 0, CHUNK)
    cum = ld(cum_ref, 0, CHUNK)
    v = ld(v_ref, 0, CHUNK)
    X, Y = (kk, q) if reverse else (q, kk)
    tot = ld(cum_ref, 0 if reverse else CHUNK - 1, 1)
    sub_rows = SUB if exact else FAST_SUB
    nblk = CHUNK // sub_rows
    lane = lax.broadcasted_iota(jnp.int32, (sub_rows, CHUNK), 1)
    sub = lax.broadcasted_iota(jnp.int32, (sub_rows, CHUNK), 0)

    if exact:
        blocks = []
        for I in range(nblk):
            lo = I * SUB
            Xb = X[lo:lo + SUB]
            ab = cum[lo:lo + SUB]
            blk = jnp.zeros((SUB, CHUNK), F32)
            for j in range(SUB):
                yj = ld(y_ref, lo + j, 1)
                aj = ld(cum_ref, lo + j, 1)
                col = jnp.sum(Xb * yj * jnp.exp2(dif(ab, aj)), axis=-1, keepdims=True)
                blk = jnp.where(lane == lo + j, col, blk)
            blocks.append(jnp.where(lane - lo <= sub, blk, 0.0))
        first_row_block = 1
    else:
        blocks = [None] * nblk
        first_row_block = 0

    ncol = nblk - first_row_block
    ends = [ld(cum_ref, J * sub_rows + sub_rows - 1, 1) for J in range(ncol)]
    ye = [Y[J * sub_rows:(J + 1) * sub_rows]
          * jnp.exp2(dif(ends[J], cum[J * sub_rows:(J + 1) * sub_rows])) for J in range(ncol)]
    if ncol < nblk:
        ye.append(jnp.zeros((CHUNK - ncol * sub_rows, HG_D), F32))
    ye = jnp.concatenate(ye, axis=0).astype(BF16)
    starts = [(J + first_row_block) * sub_rows for J in range(ncol)]
    xe = [X[starts[J]:] * jnp.exp2(dif(cum[starts[J]:], ends[J])) for J in range(ncol)]
    res = _nt_dot(jnp.concatenate(xe, axis=0).astype(BF16), ye)
    lane_blk = lane // sub_rows
    off = 0
    for J in range(ncol):
        for I in range(J + first_row_block, nblk):
            r0 = off + I * sub_rows - starts[J]
            piece = res[r0:r0 + sub_rows]
            blocks[I] = piece if blocks[I] is None else jnp.where(lane_blk == J, piece, blocks[I])
        off += CHUNK - starts[J]
    if not exact:
        blocks = [jnp.where(lane - I * sub_rows <= sub, blk, 0.0) for I, blk in enumerate(blocks)]
    L = jnp.concatenate(blocks, axis=0).astype(BF16)

    intra = _tn_dot(L, v) if reverse else jnp.dot(L, v, preferred_element_type=F32)
    qe = (q * jnp.exp2(cum)).astype(BF16)
    ke = (kk * jnp.exp2(tot - cum)).astype(BF16)
    return intra, _tn_dot(v, ke), qe, jnp.exp2(tot)


def _decay_is_bounded(cum_ref, nchunk):
    worst = jnp.zeros((1, HG_W), F32)
    for c in range(nchunk):
        for lo in range(0, CHUNK, FAST_SUB):
            first = cum_ref[0, c, lo:lo + 1, :]
            last = cum_ref[0, c, lo + FAST_SUB - 1:lo + FAST_SUB, :]
            worst = jnp.maximum(worst, jnp.abs(first - last))
    return jnp.max(worst) <= MAX_BLOCK_DECAY_LOG2


def _hgrn_tile(q_ref, cum_ref, kk_ref, v_ref, st_ref, slots, emit, reverse, nchunk, exact):
    def chunk_of(ci):
        return (nchunk - 1 - ci) if reverse else ci

    o_buf, u_buf, q_buf, d_buf = slots

    def intra(ci):
        for h in range(HG_HEADS):
            lanes = slice(h * HG_D, (h + 1) * HG_D)
            o, u, qe, dec = _hgrn_intra(q_ref, kk_ref, cum_ref, v_ref, chunk_of(ci), lanes, reverse,
                                        exact)
            o_buf[ci, h] = o
            u_buf[ci, h] = u
            q_buf[ci, h] = qe
            d_buf[ci, h] = jnp.broadcast_to(dec, d_buf.shape[2:])

    def inter(ci):
        for h in range(HG_HEADS):
            lanes = slice(h * HG_D, (h + 1) * HG_D)
            st = st_ref[h]
            emit(chunk_of(ci), lanes, o_buf[ci, h] + _nt_dot(q_buf[ci, h], st.astype(BF16)))
            st_ref[h] = st * d_buf[ci, h][0:1] + u_buf[ci, h]

    intra(0)
    for ci in range(nchunk):
        if ci + 1 < nchunk:
            intra(ci + 1)
        inter(ci)


def _gated_norm(out, of_ref, sg_ref, gw_ref, c, lanes):
    o = out + of_ref[0, c, :, lanes]
    ms = jnp.mean(o * o, axis=-1, keepdims=True)
    return o * lax.rsqrt(ms + EPS) * gw_ref[...] * sg_ref[0, c, :, lanes]


def _hgrn_kernel(*refs, reverse, tt):
    if reverse:
        q_ref, cum_ref, kk_ref, v_ref, of_ref, sg_ref, gw_ref, o_ref = refs[:8]
    else:
        q_ref, cum_ref, kk_ref, v_ref, o_ref = refs[:5]
    st_ref = refs[-5]
    slots = refs[-4:]

    @pl.when(pl.program_id(1) == 0)
    def _():
        st_ref[...] = jnp.zeros_like(st_ref)

    def emit(c, lanes, out):
        if reverse:
            out = _gated_norm(out, of_ref, sg_ref, gw_ref, c, lanes)
        o_ref[0, c, :, lanes] = out.astype(o_ref.dtype)

    nchunk = tt // CHUNK
    bounded = _decay_is_bounded(cum_ref, nchunk)

    @pl.when(bounded)
    def _():
        _hgrn_tile(q_ref, cum_ref, kk_ref, v_ref, st_ref, slots, emit, reverse, nchunk, exact=False)

    @pl.when(jnp.logical_not(bounded))
    def _():
        _hgrn_tile(q_ref, cum_ref, kk_ref, v_ref, st_ref, slots, emit, reverse, nchunk, exact=True)


def _hgrn(q, cum, kk, v, reverse, o_fwd=None, sg=None, gnorm_w=None):
    B, T, _ = q.shape
    tt = TOKEN_TILE
    nt = T // tt
    if reverse:
        tile = lambda b, i: (b, nt - 1 - i, 0, 0)
    else:
        tile = lambda b, i: (b, i, 0, 0)
    chunked = lambda a: a.reshape(B, T // CHUNK, CHUNK, HG_W)
    tok = pl.BlockSpec((1, tt // CHUNK, CHUNK, HG_W), tile)
    args = [chunked(a) for a in (q, cum, kk, v)]
    in_specs = [tok, tok, tok, tok]
    if reverse:
        args += [chunked(o_fwd), chunked(sg), gnorm_w]
        in_specs += [tok, tok, _const_spec((1, HG_D))]
    out = pl.pallas_call(
        functools.partial(_hgrn_kernel, reverse=reverse, tt=tt),
        grid=(B, nt),
        in_specs=in_specs,
        out_specs=tok,
        out_shape=jax.ShapeDtypeStruct((B, T // CHUNK, CHUNK, HG_W), BF16 if reverse else F32),
        scratch_shapes=_hgrn_scratch(tt // CHUNK),
        compiler_params=pltpu.CompilerParams(
            dimension_semantics=("parallel", "arbitrary"), vmem_limit_bytes=VMEM_LIMIT_BYTES),
        name="hgrn_bwd" if reverse else "hgrn_fwd",
    )(*args)
    return out.reshape(B, T, HG_W)


def _hgrn_scratch(nchunk):
    return [pltpu.VMEM((HG_HEADS, HG_D, HG_D), F32),
            pltpu.VMEM((nchunk, HG_HEADS, CHUNK, HG_D), F32),
            pltpu.VMEM((nchunk, HG_HEADS, HG_D, HG_D), F32),
            pltpu.VMEM((nchunk, HG_HEADS, CHUNK, HG_D), BF16),
            pltpu.VMEM((nchunk, HG_HEADS, 8, HG_D), F32)]


def _na_bias_table(rpb):
    q = np.arange(GRID_W)
    kc = np.arange(GRID_W)
    win_start = np.clip(q - NA_KW // 2, 0, GRID_W - NA_KW)
    valid = (kc[None, :] >= win_start[:, None]) & (kc[None, :] < win_start[:, None] + NA_KW)
    dc = np.clip(kc[None, :] - q[:, None], -(NA_KW - 1), NA_KW - 1) + NA_KW - 1
    onehot = jnp.asarray(dc[:, :, None] == np.arange(2 * NA_KW - 1), dtype=F32)
    toep = jnp.einsum("hrc,qkc->hrqk", rpb.astype(F32), onehot, precision=lax.Precision.HIGHEST)
    toep = jnp.where(valid[None, None], toep, MASK_VALUE)
    n_e = 2 * NA_KH - 2
    tab = jnp.stack([toep[:, 0:n_e], toep[:, 1:n_e + 1]], axis=3)
    tab = tab.reshape(NA_HEADS // 2, 2, n_e, GRID_W, 2, GRID_W)
    tab = tab.transpose(0, 2, 1, 3, 4, 5)
    return tab.reshape(NA_HEADS // 2, n_e, 2 * GRID_W, 2 * GRID_W)


def _na_kernel(q_ref, kp_ref, kc_ref, kn_ref, vp_ref, vc_ref, vn_ref, bias_ref, o_ref,
               kbuf, vbuf, sbuf0, sbuf1, *, rows):
    i = pl.program_id(1)
    tile_tokens = NA_TILE_ROWS * GRID_W
    for n, (kr, vr) in enumerate(((kp_ref, vp_ref), (kc_ref, vc_ref), (kn_ref, vn_ref))):
        kbuf[n * tile_tokens:(n + 1) * tile_tokens] = kr[0]
        vbuf[n * tile_tokens:(n + 1) * tile_tokens] = vr[0]

    lane = lax.broadcasted_iota(jnp.int32, (GRID_W, 2 * NA_DH), 1)
    first = lane < NA_DH
    npairs = NA_HEADS // 2

    def window(j):
        r = i * NA_TILE_ROWS + j
        start = jnp.clip(r - NA_KH // 2, 0, rows - NA_KH)
        koff = pl.multiple_of((start - i * NA_TILE_ROWS + NA_TILE_ROWS) * GRID_W, GRID_W)
        return r - start, pl.ds(koff, NA_KH * GRID_W), pl.ds(pl.multiple_of(j * GRID_W, GRID_W), GRID_W)

    def scores(j, sbuf):
        shift, krows, qrows = window(j)
        for p in range(npairs):
            lanes = slice(p * 2 * NA_DH, (p + 1) * 2 * NA_DH)
            qp = q_ref[0, qrows, lanes]
            zero = jnp.zeros_like(qp)
            qab = jnp.concatenate([jnp.where(first, qp, zero), jnp.where(first, zero, qp)], axis=0)
            s = _nt_dot(qab, kbuf[krows, lanes])
            for m in range(NA_KH // 2):
                cols = slice(m * 2 * GRID_W, (m + 1) * 2 * GRID_W)
                sbuf[p, :, cols] = s[:, cols] + bias_ref[p, NA_KH - 1 - shift + 2 * m]

    def attend(j, sbuf):
        _, krows, qrows = window(j)
        for p in range(npairs):
            lanes = slice(p * 2 * NA_DH, (p + 1) * 2 * NA_DH)
            s = sbuf[p]
            mx = jnp.max(s, axis=-1, keepdims=True)
            e = jnp.exp(s - mx)
            den = jnp.sum(e, axis=-1, keepdims=True)
            pv = jnp.dot(e.astype(BF16), vbuf[krows, lanes], preferred_element_type=F32) / den
            o_ref[0, qrows, lanes] = jnp.where(first, pv[:GRID_W], pv[GRID_W:]).astype(o_ref.dtype)

    sbufs = (sbuf0, sbuf1)
    scores(0, sbufs[0])
    for j in range(NA_TILE_ROWS):
        if j + 1 < NA_TILE_ROWS:
            scores(j + 1, sbufs[(j + 1) % 2])
        attend(j, sbufs[j % 2])


def _na(nq, nk, nv, bias):
    B, T, _ = nq.shape
    rows = T // GRID_W
    tile_tokens = NA_TILE_ROWS * GRID_W
    nt = rows // NA_TILE_ROWS
    cur = pl.BlockSpec((1, tile_tokens, NA_W), lambda b, i: (b, i, 0))
    prev = pl.BlockSpec((1, tile_tokens, NA_W), lambda b, i: (b, jnp.maximum(i - 1, 0), 0))
    nxt = pl.BlockSpec((1, tile_tokens, NA_W), lambda b, i: (b, jnp.minimum(i + 1, nt - 1), 0))
    return pl.pallas_call(
        functools.partial(_na_kernel, rows=rows),
        grid=(B, nt),
        in_specs=[cur, prev, cur, nxt, prev, cur, nxt, _const_spec(bias.shape)],
        out_specs=cur,
        out_shape=jax.ShapeDtypeStruct((B, T, NA_W), BF16),
        scratch_shapes=[pltpu.VMEM((3 * tile_tokens, NA_W), BF16),
                        pltpu.VMEM((3 * tile_tokens, NA_W), BF16),
                        pltpu.VMEM((NA_HEADS // 2, 2 * GRID_W, NA_KH * GRID_W), F32),
                        pltpu.VMEM((NA_HEADS // 2, 2 * GRID_W, NA_KH * GRID_W), F32)],
        compiler_params=pltpu.CompilerParams(
            dimension_semantics=("parallel", "parallel"), vmem_limit_bytes=VMEM_LIMIT_BYTES),
        name="natten",
    )(nq, nk, nk, nk, nv, nv, nv, bias)


def _ffn_kernel(x_ref, oh_ref, on_ref, wo_ref, nw_ref, wg_ref, wu_ref, wd_ref, y_ref):
    x = x_ref[0]
    x = x + jnp.dot(oh_ref[0], wo_ref[:HG_W], preferred_element_type=F32)
    x = x + jnp.dot(on_ref[0], wo_ref[HG_W:], preferred_element_type=F32)
    ms = jnp.mean(x * x, axis=-1, keepdims=True)
    h = (x * lax.rsqrt(ms + EPS) * nw_ref[...]).astype(BF16)
    gate = jnp.dot(h, wg_ref[...], preferred_element_type=F32)
    up = jnp.dot(h, wu_ref[...], preferred_element_type=F32)
    act = (_silu(gate) * up).astype(BF16)
    y_ref[0] = x + jnp.dot(act, wd_ref[...], preferred_element_type=F32)


def _ffn(x, o_hgrn, o_na, w_out, norm_w, w_gate, w_up, w_down):
    B, T, _ = x.shape
    tm = TOKEN_TILE
    tok = lambda w: pl.BlockSpec((1, tm, w), lambda b, i: (b, i, 0))
    return pl.pallas_call(
        _ffn_kernel,
        grid=(B, T // tm),
        in_specs=[tok(D_MODEL), tok(HG_W), tok(NA_W), _const_spec(w_out.shape),
                  _const_spec((1, D_MODEL)), _const_spec(w_gate.shape),
                  _const_spec(w_up.shape), _const_spec(w_down.shape)],
        out_specs=tok(D_MODEL),
        out_shape=jax.ShapeDtypeStruct(x.shape, x.dtype),
        compiler_params=pltpu.CompilerParams(
            dimension_semantics=("parallel", "parallel"), vmem_limit_bytes=VMEM_LIMIT_BYTES),
        name="outproj_ffn",
    )(x, o_hgrn, o_na, w_out, norm_w, w_gate, w_up, w_down)


def _encoder_layer(x, p):
    q, bf, cb, kf, kb, v, sg, nq, nk, nv = _inproj(
        x, p["norm_mix_w"], p["w_in"], p["lb"], p["qnw"], p["knw"], p["gmat"])
    o_fwd = _hgrn(q, bf, kf, v, reverse=False)
    o_hgrn = _hgrn(q, cb, kb, v, reverse=True, o_fwd=o_fwd, sg=sg, gnorm_w=p["gnorm_w"])
    o_na = _na(nq, nk, nv, p["na_bias"])
    return _ffn(x, o_hgrn, o_na, p["w_out"], p["norm_ffn_w"], p["w_gate"], p["w_up"], p["w_down"])


def kernel(x_prompt, x_sample, norm_mix_w, w_in, hgrn_lb, hgrn_gnorm_w, na_q_norm_w, na_k_norm_w,
           na_rpb, w_out, norm_ffn_w, w_gate, w_up, w_down):
    head_of_lane = np.arange(NA_W) // NA_DH
    p = {
        "norm_mix_w": norm_mix_w[0][None].astype(F32),
        "w_in": w_in[0].astype(BF16),
        "lb": hgrn_lb.astype(F32),
        "qnw": jnp.tile(na_q_norm_w[0].astype(F32), NA_HEADS)[None],
        "knw": jnp.tile(na_k_norm_w[0].astype(F32), NA_HEADS)[None],
        "gmat": jnp.asarray(head_of_lane[:, None] == head_of_lane[None, :], dtype=BF16),
        "gnorm_w": hgrn_gnorm_w[0][None].astype(F32),
        "na_bias": _na_bias_table(na_rpb[0]),
        "w_out": w_out[0].astype(BF16),
        "norm_ffn_w": norm_ffn_w[0][None].astype(F32),
        "w_gate": w_gate[0].astype(BF16),
        "w_up": w_up[0].astype(BF16),
        "w_down": w_down[0].astype(BF16),
    }
    return (_encoder_layer(x_prompt, p), _encoder_layer(x_sample, p))
```

```python
import functools

import numpy as np
import jax
import jax.numpy as jnp
from jax import lax
from jax.experimental import pallas as pl
from jax.experimental.pallas import tpu as pltpu

F32 = jnp.float32
BF16 = jnp.bfloat16

D_MODEL = 1024
GRID_W = 64
HG_HEADS = 4
HG_D = 128
HG_W = HG_HEADS * HG_D
CHUNK = 64
SUB = 8
FAST_SUB = 16
MAX_BLOCK_DECAY_LOG2 = 80.0
NA_HEADS = 8
NA_DH = 64
NA_W = NA_HEADS * NA_DH
NA_KH = 8
NA_KW = 16
D_FF = 2816
EPS = 1e-6
MASK_VALUE = -1e30

TOKEN_TILE = 512
NA_TILE_ROWS = 8
VMEM_LIMIT_BYTES = 56 * 1024 * 1024


def _silu(x):
    return x * jax.nn.sigmoid(x)


def _const_spec(shape):
    return pl.BlockSpec(shape, lambda *_: (0,) * len(shape), pipeline_mode=pl.Buffered(1))


def _chunk_scan(g, rowmod, reverse):
    n = g.shape[0]
    k = 1
    while k < CHUNK:
        if reverse:
            shifted = pltpu.roll(g, n - k, 0)
            g = g + jnp.where(rowmod < CHUNK - k, shifted, 0.0)
        else:
            shifted = pltpu.roll(g, k, 0)
            g = g + jnp.where(rowmod >= k, shifted, 0.0)
        k *= 2
    return g


def _inproj_kernel(x_ref, nw_ref, w_ref, qnw_ref, knw_ref, gmat_ref,
                   q_ref, ff_ref, fb_ref, v_ref, sg_ref, nq_ref, nk_ref, nv_ref):
    x = x_ref[0]
    ms = jnp.mean(x * x, axis=-1, keepdims=True)
    h = (x * lax.rsqrt(ms + EPS) * nw_ref[...]).astype(BF16)

    def head_norm(a, w):
        ssum = jnp.dot((a * a).astype(BF16), gmat_ref[...], preferred_element_type=F32)
        return a * lax.rsqrt(ssum * (1.0 / NA_DH) + EPS) * w

    post = [
        (q_ref, lambda a: a),
        (ff_ref, lambda a: a),
        (fb_ref, lambda a: a),
        (v_ref, lambda a: a),
        (sg_ref, _silu),
        (nq_ref, lambda a: head_norm(a, qnw_ref[...]) * (NA_DH ** -0.5)),
        (nk_ref, lambda a: head_norm(a, knw_ref[...])),
        (nv_ref, lambda a: a),
    ]
    for c, (ref, fn) in enumerate(post):
        ref[0] = fn(jnp.dot(h, w_ref[c], preferred_element_type=F32)).astype(ref.dtype)


def _inproj(x, norm_w, w_in, qnw, knw, gmat):
    B, T, _ = x.shape
    tm = TOKEN_TILE
    tok = lambda w: pl.BlockSpec((1, tm, w), lambda b, i: (b, i, 0))
    f32_out = jax.ShapeDtypeStruct((B, T, HG_W), F32)
    bf16_out = jax.ShapeDtypeStruct((B, T, HG_W), BF16)
    return pl.pallas_call(
        _inproj_kernel,
        grid=(B, T // tm),
        in_specs=[tok(D_MODEL), _const_spec((1, D_MODEL)), _const_spec(w_in.shape),
                  _const_spec((1, NA_W)), _const_spec((1, NA_W)), _const_spec((NA_W, NA_W))],
        out_specs=[tok(HG_W)] * 8,
        out_shape=[f32_out, f32_out, f32_out, bf16_out, f32_out, bf16_out, bf16_out, bf16_out],
        compiler_params=pltpu.CompilerParams(
            dimension_semantics=("parallel", "parallel"), vmem_limit_bytes=VMEM_LIMIT_BYTES),
        name="inproj",
    )(x, norm_w, w_in, qnw, knw, gmat)


def _nt_dot(a, b):
    return lax.dot_general(a, b, (((1,), (1,)), ((), ())), preferred_element_type=F32)


def _tn_dot(a, b):
    return lax.dot_general(a, b, (((0,), (0,)), ((), ())), preferred_element_type=F32)


def _hgrn_intra(q_ref, kk_ref, cum_ref, v_ref, c, lanes, reverse, exact):
    def ld(ref, lo, n):
        return ref[c, lo:lo + n, lanes]

    def dif(u, w):
        return (w - u) if reverse else (u - w)

    x_ref, y_ref = (kk_ref, q_ref) if reverse else (q_ref, kk_ref)
    q = ld(q_ref, 0, CHUNK)
    kk = ld(kk_ref, 0, CHUNK)
    cum = ld(cum_ref, 0, CHUNK)
    v = v_ref[0, c, :, lanes]
    X, Y = (kk, q) if reverse else (q, kk)
    tot = ld(cum_ref, 0 if reverse else CHUNK - 1, 1)
    sub_rows = SUB if exact else FAST_SUB
    nblk = CHUNK // sub_rows
    lane = lax.broadcasted_iota(jnp.int32, (sub_rows, CHUNK), 1)
    sub = lax.broadcasted_iota(jnp.int32, (sub_rows, CHUNK), 0)

    if exact:
        blocks = []
        for I in range(nblk):
            lo = I * SUB
            Xb = X[lo:lo + SUB]
            ab = cum[lo:lo + SUB]
            blk = jnp.zeros((SUB, CHUNK), F32)
            for j in range(SUB):
                yj = ld(y_ref, lo + j, 1)
                aj = ld(cum_ref, lo + j, 1)
                col = jnp.sum(Xb * yj * jnp.exp2(dif(ab, aj)), axis=-1, keepdims=True)
                blk = jnp.where(lane == lo + j, col, blk)
            blocks.append(jnp.where(lane - lo <= sub, blk, 0.0))
        first_row_block = 1
    else:
        blocks = [None] * nblk
        first_row_block = 0

    ncol = nblk - first_row_block
    ends = [ld(cum_ref, J * sub_rows + sub_rows - 1, 1) for J in range(ncol)]
    ye = [Y[J * sub_rows:(J + 1) * sub_rows]
          * jnp.exp2(dif(ends[J], cum[J * sub_rows:(J + 1) * sub_rows])) for J in range(ncol)]
    if ncol < nblk:
        ye.append(jnp.zeros((CHUNK - ncol * sub_rows, HG_D), F32))
    ye = jnp.concatenate(ye, axis=0).astype(BF16)
    starts = [(J + first_row_block) * sub_rows for J in range(ncol)]
    xe = [X[starts[J]:] * jnp.exp2(dif(cum[starts[J]:], ends[J])) for J in range(ncol)]
    res = _nt_dot(jnp.concatenate(xe, axis=0).astype(BF16), ye)
    lane_blk = lane // sub_rows
    off = 0
    for J in range(ncol):
        for I in range(J + first_row_block, nblk):
            r0 = off + I * sub_rows - starts[J]
            piece = res[r0:r0 + sub_rows]
            blocks[I] = piece if blocks[I] is None else jnp.where(lane_blk == J, piece, blocks[I])
        off += CHUNK - starts[J]
    if not exact:
        blocks = [jnp.where(lane - I * sub_rows <= sub, blk, 0.0) for I, blk in enumerate(blocks)]
    L = jnp.concatenate(blocks, axis=0).astype(BF16)

    intra = _tn_dot(L, v) if reverse else jnp.dot(L, v, preferred_element_type=F32)
    qe = (q * jnp.exp2(cum)).astype(BF16)
    ke = (kk * jnp.exp2(tot - cum)).astype(BF16)
    return intra, _tn_dot(v, ke), qe, jnp.exp2(tot)


def _gate_lower_bound(lb_ref, reverse):
    d = 1 if reverse else 0
    a0 = lb_ref[0, d:d + 1, :]
    a1 = lb_ref[1, d:d + 1, :]
    m = jnp.maximum(a0, a1)
    e0 = jnp.exp(a0 - m)
    e1 = jnp.exp(a1 - m)
    return e0 / (e0 + e1)


def _hgrn_tile(qr_ref, fl_ref, v_ref, lb, st_ref, gates, slots, emit, reverse, nchunk, exact):
    def chunk_of(ci):
        return (nchunk - 1 - ci) if reverse else ci

    q_ref, kk_ref, cum_ref = gates
    o_buf, u_buf, q_buf, d_buf = slots
    row = lax.broadcasted_iota(jnp.int32, (CHUNK, HG_W), 0)

    def intra(ci):
        c = chunk_of(ci)
        f = lb + (1.0 - lb) * jax.nn.sigmoid(fl_ref[0, c])
        q_ref[c] = _silu(qr_ref[0, c])
        kk_ref[c] = 1.0 - f
        cum_ref[c] = _chunk_scan(jnp.log2(f), row, reverse)
        for h in range(HG_HEADS):
            lanes = slice(h * HG_D, (h + 1) * HG_D)
            o, u, qe, dec = _hgrn_intra(q_ref, kk_ref, cum_ref, v_ref, c, lanes, reverse, exact)
            o_buf[ci, h] = o
            u_buf[ci, h] = u
            q_buf[ci, h] = qe
            d_buf[ci, h] = jnp.broadcast_to(dec, d_buf.shape[2:])

    def inter(ci):
        for h in range(HG_HEADS):
            lanes = slice(h * HG_D, (h + 1) * HG_D)
            st = st_ref[h]
            emit(chunk_of(ci), lanes, o_buf[ci, h] + _nt_dot(q_buf[ci, h], st.astype(BF16)))
            st_ref[h] = st * d_buf[ci, h][0:1] + u_buf[ci, h]

    intra(0)
    for ci in range(nchunk):
        if ci + 1 < nchunk:
            intra(ci + 1)
        inter(ci)


def _gated_norm(out, of_ref, sg_ref, gw_ref, c, lanes):
    o = out + of_ref[0, c, :, lanes]
    ms = jnp.mean(o * o, axis=-1, keepdims=True)
    return o * lax.rsqrt(ms + EPS) * gw_ref[...] * sg_ref[0, c, :, lanes]


def _hgrn_kernel(*refs, reverse, tt):
    if reverse:
        qr_ref, fl_ref, v_ref, lb_ref, of_ref, sg_ref, gw_ref, o_ref = refs[:8]
    else:
        qr_ref, fl_ref, v_ref, lb_ref, o_ref = refs[:5]
    st_ref = refs[-8]
    gates = refs[-7:-4]
    slots = refs[-4:]

    @pl.when(pl.program_id(1) == 0)
    def _():
        st_ref[...] = jnp.zeros_like(st_ref)

    def emit(c, lanes, out):
        if reverse:
            out = _gated_norm(out, of_ref, sg_ref, gw_ref, c, lanes)
        o_ref[0, c, :, lanes] = out.astype(o_ref.dtype)

    nchunk = tt // CHUNK
    lb = _gate_lower_bound(lb_ref, reverse)
    bounded = jnp.min(lb) >= 2.0 ** (-MAX_BLOCK_DECAY_LOG2 / (FAST_SUB - 1))

    @pl.when(bounded)
    def _():
        _hgrn_tile(qr_ref, fl_ref, v_ref, lb, st_ref, gates, slots, emit, reverse, nchunk, exact=False)

    @pl.when(jnp.logical_not(bounded))
    def _():
        _hgrn_tile(qr_ref, fl_ref, v_ref, lb, st_ref, gates, slots, emit, reverse, nchunk, exact=True)


def _hgrn(q_raw, f_logit, v, lb_param, reverse, o_fwd=None, sg=None, gnorm_w=None):
    B, T, _ = q_raw.shape
    tt = TOKEN_TILE
    nt = T // tt
    if reverse:
        tile = lambda b, i: (b, nt - 1 - i, 0, 0)
    else:
        tile = lambda b, i: (b, i, 0, 0)
    chunked = lambda a: a.reshape(B, T // CHUNK, CHUNK, HG_W)
    tok = pl.BlockSpec((1, tt // CHUNK, CHUNK, HG_W), tile)
    args = [chunked(q_raw), chunked(f_logit), chunked(v), lb_param]
    in_specs = [tok, tok, tok, _const_spec(lb_param.shape)]
    if reverse:
        args += [chunked(o_fwd), chunked(sg), gnorm_w]
        in_specs += [tok, tok, _const_spec((1, HG_D))]
    out = pl.pallas_call(
        functools.partial(_hgrn_kernel, reverse=reverse, tt=tt),
        grid=(B, nt),
        in_specs=in_specs,
        out_specs=tok,
        out_shape=jax.ShapeDtypeStruct((B, T // CHUNK, CHUNK, HG_W), BF16 if reverse else F32),
        scratch_shapes=_hgrn_scratch(tt // CHUNK),
        compiler_params=pltpu.CompilerParams(
            dimension_semantics=("parallel", "arbitrary"), vmem_limit_bytes=VMEM_LIMIT_BYTES),
        name="hgrn_bwd" if reverse else "hgrn_fwd",
    )(*args)
    return out.reshape(B, T, HG_W)


def _hgrn_scratch(nchunk):
    return [pltpu.VMEM((HG_HEADS, HG_D, HG_D), F32)] + 3 * [
            pltpu.VMEM((nchunk, CHUNK, HG_W), F32)] + [
            pltpu.VMEM((nchunk, HG_HEADS, CHUNK, HG_D), F32),
            pltpu.VMEM((nchunk, HG_HEADS, HG_D, HG_D), F32),
            pltpu.VMEM((nchunk, HG_HEADS, CHUNK, HG_D), BF16),
            pltpu.VMEM((nchunk, HG_HEADS, 8, HG_D), F32)]


def _na_bias_table(rpb):
    q = np.arange(GRID_W)
    kc = np.arange(GRID_W)
    win_start = np.clip(q - NA_KW // 2, 0, GRID_W - NA_KW)
    valid = (kc[None, :] >= win_start[:, None]) & (kc[None, :] < win_start[:, None] + NA_KW)
    dc = np.clip(kc[None, :] - q[:, None], -(NA_KW - 1), NA_KW - 1) + NA_KW - 1
    onehot = jnp.asarray(dc[:, :, None] == np.arange(2 * NA_KW - 1), dtype=F32)
    toep = jnp.einsum("hrc,qkc->hrqk", rpb.astype(F32), onehot, precision=lax.Precision.HIGHEST)
    toep = jnp.where(valid[None, None], toep, MASK_VALUE)
    n_e = 2 * NA_KH - 2
    tab = jnp.stack([toep[:, 0:n_e], toep[:, 1:n_e + 1]], axis=3)
    tab = tab.reshape(NA_HEADS // 2, 2, n_e, GRID_W, 2, GRID_W)
    tab = tab.transpose(0, 2, 1, 3, 4, 5)
    return tab.reshape(NA_HEADS // 2, n_e, 2 * GRID_W, 2 * GRID_W)


def _na_kernel(q_ref, kp_ref, kc_ref, kn_ref, vp_ref, vc_ref, vn_ref, bias_ref, o_ref,
               kbuf, vbuf, sbuf0, sbuf1, *, rows):
    i = pl.program_id(1)
    tile_tokens = NA_TILE_ROWS * GRID_W
    for n, (kr, vr) in enumerate(((kp_ref, vp_ref), (kc_ref, vc_ref), (kn_ref, vn_ref))):
        kbuf[n * tile_tokens:(n + 1) * tile_tokens] = kr[0]
        vbuf[n * tile_tokens:(n + 1) * tile_tokens] = vr[0]

    lane = lax.broadcasted_iota(jnp.int32, (GRID_W, 2 * NA_DH), 1)
    first = lane < NA_DH
    npairs = NA_HEADS // 2

    def window(j):
        r = i * NA_TILE_ROWS + j
        start = jnp.clip(r - NA_KH // 2, 0, rows - NA_KH)
        koff = pl.multiple_of((start - i * NA_TILE_ROWS + NA_TILE_ROWS) * GRID_W, GRID_W)
        return r - start, pl.ds(koff, NA_KH * GRID_W), pl.ds(pl.multiple_of(j * GRID_W, GRID_W), GRID_W)

    def scores(j, sbuf):
        shift, krows, qrows = window(j)
        for p in range(npairs):
            lanes = slice(p * 2 * NA_DH, (p + 1) * 2 * NA_DH)
            qp = q_ref[0, qrows, lanes]
            zero = jnp.zeros_like(qp)
            qab = jnp.concatenate([jnp.where(first, qp, zero), jnp.where(first, zero, qp)], axis=0)
            s = _nt_dot(qab, kbuf[krows, lanes])
            for m in range(NA_KH // 2):
                cols = slice(m * 2 * GRID_W, (m + 1) * 2 * GRID_W)
                sbuf[p, :, cols] = s[:, cols] + bias_ref[p, NA_KH - 1 - shift + 2 * m]

    def attend(j, sbuf):
        _, krows, qrows = window(j)
        for p in range(npairs):
            lanes = slice(p * 2 * NA_DH, (p + 1) * 2 * NA_DH)
            s = sbuf[p]
            mx = jnp.max(s, axis=-1, keepdims=True)
            e = jnp.exp(s - mx)
            den = jnp.sum(e, axis=-1, keepdims=True)
            pv = jnp.dot(e.astype(BF16), vbuf[krows, lanes], preferred_element_type=F32) / den
            o_ref[0, qrows, lanes] = jnp.where(first, pv[:GRID_W], pv[GRID_W:]).astype(o_ref.dtype)

    sbufs = (sbuf0, sbuf1)
    scores(0, sbufs[0])
    for j in range(NA_TILE_ROWS):
        if j + 1 < NA_TILE_ROWS:
            scores(j + 1, sbufs[(j + 1) % 2])
        attend(j, sbufs[j % 2])


def _na(nq, nk, nv, bias):
    B, T, _ = nq.shape
    rows = T // GRID_W
    tile_tokens = NA_TILE_ROWS * GRID_W
    nt = rows // NA_TILE_ROWS
    cur = pl.BlockSpec((1, tile_tokens, NA_W), lambda b, i: (b, i, 0))
    prev = pl.BlockSpec((1, tile_tokens, NA_W), lambda b, i: (b, jnp.maximum(i - 1, 0), 0))
    nxt = pl.BlockSpec((1, tile_tokens, NA_W), lambda b, i: (b, jnp.minimum(i + 1, nt - 1), 0))
    return pl.pallas_call(
        functools.partial(_na_kernel, rows=rows),
        grid=(B, nt),
        in_specs=[cur, prev, cur, nxt, prev, cur, nxt, _const_spec(bias.shape)],
        out_specs=cur,
        out_shape=jax.ShapeDtypeStruct((B, T, NA_W), BF16),
        scratch_shapes=[pltpu.VMEM((3 * tile_tokens, NA_W), BF16),
                        pltpu.VMEM((3 * tile_tokens, NA_W), BF16),
                        pltpu.VMEM((NA_HEADS // 2, 2 * GRID_W, NA_KH * GRID_W), F32),
                        pltpu.VMEM((NA_HEADS // 2, 2 * GRID_W, NA_KH * GRID_W), F32)],
        compiler_params=pltpu.CompilerParams(
            dimension_semantics=("parallel", "parallel"), vmem_limit_bytes=VMEM_LIMIT_BYTES),
        name="natten",
    )(nq, nk, nk, nk, nv, nv, nv, bias)


def _ffn_kernel(x_ref, oh_ref, on_ref, wo_ref, nw_ref, wg_ref, wu_ref, wd_ref, y_ref):
    x = x_ref[0]
    x = x + jnp.dot(oh_ref[0], wo_ref[:HG_W], preferred_element_type=F32)
    x = x + jnp.dot(on_ref[0], wo_ref[HG_W:], preferred_element_type=F32)
    ms = jnp.mean(x * x, axis=-1, keepdims=True)
    h = (x * lax.rsqrt(ms + EPS) * nw_ref[...]).astype(BF16)
    gate = jnp.dot(h, wg_ref[...], preferred_element_type=F32)
    up = jnp.dot(h, wu_ref[...], preferred_element_type=F32)
    act = (_silu(gate) * up).astype(BF16)
    y_ref[0] = x + jnp.dot(act, wd_ref[...], preferred_element_type=F32)


def _ffn(x, o_hgrn, o_na, w_out, norm_w, w_gate, w_up, w_down):
    B, T, _ = x.shape
    tm = TOKEN_TILE
    tok = lambda w: pl.BlockSpec((1, tm, w), lambda b, i: (b, i, 0))
    return pl.pallas_call(
        _ffn_kernel,
        grid=(B, T // tm),
        in_specs=[tok(D_MODEL), tok(HG_W), tok(NA_W), _const_spec(w_out.shape),
                  _const_spec((1, D_MODEL)), _const_spec(w_gate.shape),
                  _const_spec(w_up.shape), _const_spec(w_down.shape)],
        out_specs=tok(D_MODEL),
        out_shape=jax.ShapeDtypeStruct(x.shape, x.dtype),
        compiler_params=pltpu.CompilerParams(
            dimension_semantics=("parallel", "parallel"), vmem_limit_bytes=VMEM_LIMIT_BYTES),
        name="outproj_ffn",
    )(x, o_hgrn, o_na, w_out, norm_w, w_gate, w_up, w_down)


def _encoder_layer(x, p):
    q, ff, fb, v, sg, nq, nk, nv = _inproj(
        x, p["norm_mix_w"], p["w_in"], p["qnw"], p["knw"], p["gmat"])
    o_fwd = _hgrn(q, ff, v, p["lb"], reverse=False)
    o_hgrn = _hgrn(q, fb, v, p["lb"], reverse=True, o_fwd=o_fwd, sg=sg, gnorm_w=p["gnorm_w"])
    o_na = _na(nq, nk, nv, p["na_bias"])
    return _ffn(x, o_hgrn, o_na, p["w_out"], p["norm_ffn_w"], p["w_gate"], p["w_up"], p["w_down"])


def kernel(x_prompt, x_sample, norm_mix_w, w_in, hgrn_lb, hgrn_gnorm_w, na_q_norm_w, na_k_norm_w,
           na_rpb, w_out, norm_ffn_w, w_gate, w_up, w_down):
    head_of_lane = np.arange(NA_W) // NA_DH
    p = {
        "norm_mix_w": norm_mix_w[0][None].astype(F32),
        "w_in": w_in[0].astype(BF16).reshape(D_MODEL, -1, HG_W).transpose(1, 0, 2),
        "lb": hgrn_lb.astype(F32),
        "qnw": jnp.tile(na_q_norm_w[0].astype(F32), NA_HEADS)[None],
        "knw": jnp.tile(na_k_norm_w[0].astype(F32), NA_HEADS)[None],
        "gmat": jnp.asarray(head_of_lane[:, None] == head_of_lane[None, :], dtype=BF16),
        "gnorm_w": hgrn_gnorm_w[0][None].astype(F32),
        "na_bias": _na_bias_table(na_rpb[0]),
        "w_out": w_out[0].astype(BF16),
        "norm_ffn_w": norm_ffn_w[0][None].astype(F32),
        "w_gate": w_gate[0].astype(BF16),
        "w_up": w_up[0].astype(BF16),
        "w_down": w_down[0].astype(BF16),
    }
    return (_encoder_layer(x_prompt, p), _encoder_layer(x_sample, p))
```

```python
import functools

import numpy as np
import jax
import jax.numpy as jnp
from jax import lax
from jax.experimental import pallas as pl
from jax.experimental.pallas import tpu as pltpu

F32 = jnp.float32
BF16 = jnp.bfloat16

D_MODEL = 1024
GRID_W = 64
HG_HEADS = 4
HG_D = 128
HG_W = HG_HEADS * HG_D
CHUNK = 64
SUB = 8
FAST_SUB = 16
MAX_BLOCK_DECAY_LOG2 = 80.0
NA_HEADS = 8
NA_DH = 64
NA_W = NA_HEADS * NA_DH
NA_KH = 8
NA_KW = 16
D_FF = 2816
EPS = 1e-6
MASK_VALUE = -1e30

TOKEN_TILE = 512
NA_TILE_ROWS = 8
NA_SPAN_ROWS = NA_TILE_ROWS + NA_KH
VMEM_LIMIT_BYTES = 56 * 1024 * 1024


def _silu(x):
    return x * jax.nn.sigmoid(x)


def _const_spec(shape):
    return pl.BlockSpec(shape, lambda *_: (0,) * len(shape), pipeline_mode=pl.Buffered(1))


def _chunk_scan(g, rowmod, reverse):
    n = g.shape[0]
    k = 1
    while k < CHUNK:
        if reverse:
            shifted = pltpu.roll(g, n - k, 0)
            g = g + jnp.where(rowmod < CHUNK - k, shifted, 0.0)
        else:
            shifted = pltpu.roll(g, k, 0)
            g = g + jnp.where(rowmod >= k, shifted, 0.0)
        k *= 2
    return g


N_PROJ_BLOCKS = 8


def _inproj_kernel(x_ref, nw_ref, qnw_ref, knw_ref, gmat_ref, *refs):
    w_refs = refs[:N_PROJ_BLOCKS]
    q_ref, ff_ref, fb_ref, v_ref, sg_ref, nq_ref, nk_ref, nv_ref = refs[N_PROJ_BLOCKS:]
    x = x_ref[0]
    ms = jnp.mean(x * x, axis=-1, keepdims=True)
    h = (x * lax.rsqrt(ms + EPS) * nw_ref[...]).astype(BF16)

    def head_norm(a, w):
        ssum = jnp.dot((a * a).astype(BF16), gmat_ref[...], preferred_element_type=F32)
        return a * lax.rsqrt(ssum * (1.0 / NA_DH) + EPS) * w

    post = [
        (q_ref, lambda a: a),
        (ff_ref, lambda a: a),
        (fb_ref, lambda a: a),
        (v_ref, lambda a: a),
        (sg_ref, _silu),
        (nq_ref, lambda a: head_norm(a, qnw_ref[...]) * (NA_DH ** -0.5)),
        (nk_ref, lambda a: head_norm(a, knw_ref[...])),
        (nv_ref, lambda a: a),
    ]
    for w_ref, (ref, fn) in zip(w_refs, post):
        ref[0] = fn(jnp.dot(h, w_ref[...], preferred_element_type=F32)).astype(ref.dtype)


def _inproj(x, norm_w, w_in, qnw, knw, gmat):
    B, T, _ = x.shape
    tm = TOKEN_TILE
    tok = lambda w: pl.BlockSpec((1, tm, w), lambda b, i: (b, i, 0))
    w_block = lambda c: pl.BlockSpec((D_MODEL, HG_W), lambda b, i: (0, c),
                                     pipeline_mode=pl.Buffered(1))
    f32_out = jax.ShapeDtypeStruct((B, T, HG_W), F32)
    bf16_out = jax.ShapeDtypeStruct((B, T, HG_W), BF16)
    return pl.pallas_call(
        _inproj_kernel,
        grid=(B, T // tm),
        in_specs=[tok(D_MODEL), _const_spec((1, D_MODEL)), _const_spec((1, NA_W)),
                  _const_spec((1, NA_W)), _const_spec((NA_W, NA_W))]
                 + [w_block(c) for c in range(N_PROJ_BLOCKS)],
        out_specs=[tok(HG_W)] * 8,
        out_shape=[f32_out, f32_out, f32_out, bf16_out, f32_out, bf16_out, bf16_out, bf16_out],
        compiler_params=pltpu.CompilerParams(
            dimension_semantics=("parallel", "parallel"), vmem_limit_bytes=VMEM_LIMIT_BYTES),
        name="inproj",
    )(x, norm_w, qnw, knw, gmat, *([w_in] * N_PROJ_BLOCKS))


def _nt_dot(a, b):
    return lax.dot_general(a, b, (((1,), (1,)), ((), ())), preferred_element_type=F32)


def _tn_dot(a, b):
    return lax.dot_general(a, b, (((0,), (0,)), ((), ())), preferred_element_type=F32)


def _hgrn_intra(q_ref, kk_ref, cum_ref, v_ref, c, lanes, reverse, exact):
    def ld(ref, lo, n):
        return ref[c, lo:lo + n, lanes]

    def dif(u, w):
        return (w - u) if reverse else (u - w)

    x_ref, y_ref = (kk_ref, q_ref) if reverse else (q_ref, kk_ref)
    q = ld(q_ref, 0, CHUNK)
    kk = ld(kk_ref, 0, CHUNK)
    cum = ld(cum_ref, 0, CHUNK)
    v = v_ref[0, c, :, lanes]
    X, Y = (kk, q) if reverse else (q, kk)
    tot = ld(cum_ref, 0 if reverse else CHUNK - 1, 1)
    sub_rows = SUB if exact else FAST_SUB
    nblk = CHUNK // sub_rows
    lane = lax.broadcasted_iota(jnp.int32, (sub_rows, CHUNK), 1)
    sub = lax.broadcasted_iota(jnp.int32, (sub_rows, CHUNK), 0)

    if exact:
        blocks = []
        for I in range(nblk):
            lo = I * SUB
            Xb = X[lo:lo + SUB]
            ab = cum[lo:lo + SUB]
            blk = jnp.zeros((SUB, CHUNK), F32)
            for j in range(SUB):
                yj = ld(y_ref, lo + j, 1)
                aj = ld(cum_ref, lo + j, 1)
                col = jnp.sum(Xb * yj * jnp.exp2(dif(ab, aj)), axis=-1, keepdims=True)
                blk = jnp.where(lane == lo + j, col, blk)
            blocks.append(jnp.where(lane - lo <= sub, blk, 0.0))
        first_row_block = 1
    else:
        blocks = [None] * nblk
        first_row_block = 0

    ncol = nblk - first_row_block
    ends = [ld(cum_ref, J * sub_rows + sub_rows - 1, 1) for J in range(ncol)]
    ye = [Y[J * sub_rows:(J + 1) * sub_rows]
          * jnp.exp2(dif(ends[J], cum[J * sub_rows:(J + 1) * sub_rows])) for J in range(ncol)]
    if ncol < nblk:
        ye.append(jnp.zeros((CHUNK - ncol * sub_rows, HG_D), F32))
    ye = jnp.concatenate(ye, axis=0).astype(BF16)
    starts = [(J + first_row_block) * sub_rows for J in range(ncol)]
    xe = [X[starts[J]:] * jnp.exp2(dif(cum[starts[J]:], ends[J])) for J in range(ncol)]
    res = _nt_dot(jnp.concatenate(xe, axis=0).astype(BF16), ye)
    lane_blk = lane // sub_rows
    off = 0
    for J in range(ncol):
        for I in range(J + first_row_block, nblk):
            r0 = off + I * sub_rows - starts[J]
            piece = res[r0:r0 + sub_rows]
            blocks[I] = piece if blocks[I] is None else jnp.where(lane_blk == J, piece, blocks[I])
        off += CHUNK - starts[J]
    if not exact:
        blocks = [jnp.where(lane - I * sub_rows <= sub, blk, 0.0) for I, blk in enumerate(blocks)]
    L = jnp.concatenate(blocks, axis=0).astype(BF16)

    intra = _tn_dot(L, v) if reverse else jnp.dot(L, v, preferred_element_type=F32)
    qe = (q * jnp.exp2(cum)).astype(BF16)
    ke = (kk * jnp.exp2(tot - cum)).astype(BF16)
    decay = jnp.transpose(jnp.broadcast_to(jnp.exp2(tot), (HG_D, HG_D)))
    return intra, _tn_dot(ke, v), qe, decay


def _gate_lower_bound(lb_ref, reverse):
    d = 1 if reverse else 0
    a0 = lb_ref[0, d:d + 1, :]
    a1 = lb_ref[1, d:d + 1, :]
    m = jnp.maximum(a0, a1)
    e0 = jnp.exp(a0 - m)
    e1 = jnp.exp(a1 - m)
    return e0 / (e0 + e1)


def _hgrn_tile(qr_ref, fl_ref, v_ref, lb, st_ref, gates, slots, emit, reverse, nchunk, exact):
    def chunk_of(ci):
        return (nchunk - 1 - ci) if reverse else ci

    q_ref, kk_ref, cum_ref = gates
    o_buf, u_buf, q_buf, d_buf = slots
    row = lax.broadcasted_iota(jnp.int32, (CHUNK, HG_W), 0)

    def intra(ci):
        c = chunk_of(ci)
        f = lb + (1.0 - lb) * jax.nn.sigmoid(fl_ref[0, c])
        q_ref[c] = _silu(qr_ref[0, c])
        kk_ref[c] = 1.0 - f
        cum_ref[c] = _chunk_scan(jnp.log2(f), row, reverse)
        for h in range(HG_HEADS):
            lanes = slice(h * HG_D, (h + 1) * HG_D)
            o, u, qe, dec = _hgrn_intra(q_ref, kk_ref, cum_ref, v_ref, c, lanes, reverse, exact)
            o_buf[ci, h] = o
            u_buf[ci, h] = u
            q_buf[ci, h] = qe
            d_buf[ci, h] = dec

    def inter(ci):
        for h in range(HG_HEADS):
            lanes = slice(h * HG_D, (h + 1) * HG_D)
            st = st_ref[h]
            emit(chunk_of(ci), lanes,
                 o_buf[ci, h] + jnp.dot(q_buf[ci, h], st.astype(BF16), preferred_element_type=F32))
            st_ref[h] = st * d_buf[ci, h] + u_buf[ci, h]

    intra(0)
    for ci in range(nchunk):
        if ci + 1 < nchunk:
            intra(ci + 1)
        inter(ci)


def _gated_norm(out, of_ref, sg_ref, gw_ref, c, lanes):
    o = out + of_ref[0, c, :, lanes]
    ms = jnp.mean(o * o, axis=-1, keepdims=True)
    return o * lax.rsqrt(ms + EPS) * gw_ref[...] * sg_ref[0, c, :, lanes]


def _hgrn_kernel(*refs, reverse, tt):
    if reverse:
        qr_ref, fl_ref, v_ref, lb_ref, of_ref, sg_ref, gw_ref, o_ref = refs[:8]
    else:
        qr_ref, fl_ref, v_ref, lb_ref, o_ref = refs[:5]
    st_ref = refs[-8]
    gates = refs[-7:-4]
    slots = refs[-4:]

    @pl.when(pl.program_id(1) == 0)
    def _():
        st_ref[...] = jnp.zeros_like(st_ref)

    def emit(c, lanes, out):
        if reverse:
            out = _gated_norm(out, of_ref, sg_ref, gw_ref, c, lanes)
        o_ref[0, c, :, lanes] = out.astype(o_ref.dtype)

    nchunk = tt // CHUNK
    lb = _gate_lower_bound(lb_ref, reverse)
    bounded = jnp.min(lb) >= 2.0 ** (-MAX_BLOCK_DECAY_LOG2 / (FAST_SUB - 1))

    @pl.when(bounded)
    def _():
        _hgrn_tile(qr_ref, fl_ref, v_ref, lb, st_ref, gates, slots, emit, reverse, nchunk, exact=False)

    @pl.when(jnp.logical_not(bounded))
    def _():
        _hgrn_tile(qr_ref, fl_ref, v_ref, lb, st_ref, gates, slots, emit, reverse, nchunk, exact=True)


def _hgrn(q_raw, f_logit, v, lb_param, reverse, o_fwd=None, sg=None, gnorm_w=None):
    B, T, _ = q_raw.shape
    tt = TOKEN_TILE
    nt = T // tt
    if reverse:
        tile = lambda b, i: (b, nt - 1 - i, 0, 0)
    else:
        tile = lambda b, i: (b, i, 0, 0)
    chunked = lambda a: a.reshape(B, T // CHUNK, CHUNK, HG_W)
    tok = pl.BlockSpec((1, tt // CHUNK, CHUNK, HG_W), tile)
    args = [chunked(q_raw), chunked(f_logit), chunked(v), lb_param]
    in_specs = [tok, tok, tok, _const_spec(lb_param.shape)]
    if reverse:
        args += [chunked(o_fwd), chunked(sg), gnorm_w]
        in_specs += [tok, tok, _const_spec((1, HG_D))]
    out = pl.pallas_call(
        functools.partial(_hgrn_kernel, reverse=reverse, tt=tt),
        grid=(B, nt),
        in_specs=in_specs,
        out_specs=tok,
        out_shape=jax.ShapeDtypeStruct((B, T // CHUNK, CHUNK, HG_W), BF16 if reverse else F32),
        scratch_shapes=_hgrn_scratch(tt // CHUNK),
        compiler_params=pltpu.CompilerParams(
            dimension_semantics=("parallel", "arbitrary"), vmem_limit_bytes=VMEM_LIMIT_BYTES),
        name="hgrn_bwd" if reverse else "hgrn_fwd",
    )(*args)
    return out.reshape(B, T, HG_W)


def _hgrn_scratch(nchunk):
    return [pltpu.VMEM((HG_HEADS, HG_D, HG_D), F32)] + 3 * [
            pltpu.VMEM((nchunk, CHUNK, HG_W), F32)] + [
            pltpu.VMEM((nchunk, HG_HEADS, CHUNK, HG_D), F32),
            pltpu.VMEM((nchunk, HG_HEADS, HG_D, HG_D), F32),
            pltpu.VMEM((nchunk, HG_HEADS, CHUNK, HG_D), BF16),
            pltpu.VMEM((nchunk, HG_HEADS, HG_D, HG_D), F32)]


def _na_bias_table(rpb):
    q = np.arange(GRID_W)
    kc = np.arange(GRID_W)
    win_start = np.clip(q - NA_KW // 2, 0, GRID_W - NA_KW)
    valid = (kc[None, :] >= win_start[:, None]) & (kc[None, :] < win_start[:, None] + NA_KW)
    dc = np.clip(kc[None, :] - q[:, None], -(NA_KW - 1), NA_KW - 1) + NA_KW - 1
    onehot = jnp.asarray(dc[:, :, None] == np.arange(2 * NA_KW - 1), dtype=F32)
    n_e = 2 * NA_KH - 2
    rows2 = jnp.stack([rpb[:, 0:n_e], rpb[:, 1:n_e + 1]], axis=2).astype(F32)
    rows2 = rows2.reshape(NA_HEADS // 2, 2, n_e, 2, 2 * NA_KW - 1)
    tab = jnp.einsum("pxegc,qkc->pexqgk", rows2, onehot, precision=lax.Precision.HIGHEST)
    tab = jnp.where(valid[None, None, None, :, None, :], tab, MASK_VALUE)
    return tab.reshape(NA_HEADS // 2, n_e, 2 * GRID_W, 2 * GRID_W)


def _na_span_start(i, rows):
    return jnp.clip(i * NA_TILE_ROWS - NA_KH // 2, 0, rows - NA_SPAN_ROWS)


def _na_kernel(q_ref, k_ref, v_ref, bias_ref, o_ref, sbuf0, sbuf1, *, rows):
    i = pl.program_id(1)
    lane = lax.broadcasted_iota(jnp.int32, (GRID_W, 2 * NA_DH), 1)
    first = lane < NA_DH
    npairs = NA_HEADS // 2
    span_start = _na_span_start(i, rows)

    def window(j):
        r = i * NA_TILE_ROWS + j
        start = jnp.clip(r - NA_KH // 2, 0, rows - NA_KH)
        koff = pl.multiple_of((start - span_start) * GRID_W, GRID_W)
        return r - start, pl.ds(koff, NA_KH * GRID_W), pl.ds(pl.multiple_of(j * GRID_W, GRID_W), GRID_W)

    def scores(j, sbuf):
        shift, krows, qrows = window(j)
        for p in range(npairs):
            lanes = slice(p * 2 * NA_DH, (p + 1) * 2 * NA_DH)
            qp = q_ref[0, qrows, lanes]
            zero = jnp.zeros_like(qp)
            qab = jnp.concatenate([jnp.where(first, qp, zero), jnp.where(first, zero, qp)], axis=0)
            s = _nt_dot(qab, k_ref[0, krows, lanes])
            for m in range(NA_KH // 2):
                cols = slice(m * 2 * GRID_W, (m + 1) * 2 * GRID_W)
                sbuf[p, :, cols] = s[:, cols] + bias_ref[p, NA_KH - 1 - shift + 2 * m]

    def attend(j, sbuf):
        _, krows, qrows = window(j)
        for p in range(npairs):
            lanes = slice(p * 2 * NA_DH, (p + 1) * 2 * NA_DH)
            s = sbuf[p]
            mx = jnp.max(s, axis=-1, keepdims=True)
            e = jnp.exp(s - mx)
            den = jnp.sum(e, axis=-1, keepdims=True)
            pv = jnp.dot(e.astype(BF16), v_ref[0, krows, lanes], preferred_element_type=F32) / den
            o_ref[0, qrows, lanes] = jnp.where(first, pv[:GRID_W], pv[GRID_W:]).astype(o_ref.dtype)

    sbufs = (sbuf0, sbuf1)
    scores(0, sbufs[0])
    for j in range(NA_TILE_ROWS):
        if j + 1 < NA_TILE_ROWS:
            scores(j + 1, sbufs[(j + 1) % 2])
        attend(j, sbufs[j % 2])


def _na(nq, nk, nv, bias):
    B, T, _ = nq.shape
    rows = T // GRID_W
    tile_tokens = NA_TILE_ROWS * GRID_W
    nt = rows // NA_TILE_ROWS
    cur = pl.BlockSpec((1, tile_tokens, NA_W), lambda b, i: (b, i, 0))
    span = pl.BlockSpec((pl.Element(1), pl.Element(NA_SPAN_ROWS * GRID_W), pl.Element(NA_W)),
                        lambda b, i: (b, _na_span_start(i, rows) * GRID_W, 0))
    return pl.pallas_call(
        functools.partial(_na_kernel, rows=rows),
        grid=(B, nt),
        in_specs=[cur, span, span, _const_spec(bias.shape)],
        out_specs=cur,
        out_shape=jax.ShapeDtypeStruct((B, T, NA_W), BF16),
        scratch_shapes=[pltpu.VMEM((NA_HEADS // 2, 2 * GRID_W, NA_KH * GRID_W), F32),
                        pltpu.VMEM((NA_HEADS // 2, 2 * GRID_W, NA_KH * GRID_W), F32)],
        compiler_params=pltpu.CompilerParams(
            dimension_semantics=("parallel", "parallel"), vmem_limit_bytes=VMEM_LIMIT_BYTES),
        name="natten",
    )(nq, nk, nv, bias)


def _ffn_kernel(x_ref, oh_ref, on_ref, wo_ref, nw_ref, wg_ref, wu_ref, wd_ref, y_ref):
    x = x_ref[0]
    x = x + jnp.dot(oh_ref[0], wo_ref[:HG_W], preferred_element_type=F32)
    x = x + jnp.dot(on_ref[0], wo_ref[HG_W:], preferred_element_type=F32)
    ms = jnp.mean(x * x, axis=-1, keepdims=True)
    h = (x * lax.rsqrt(ms + EPS) * nw_ref[...]).astype(BF16)
    gate = jnp.dot(h, wg_ref[...], preferred_element_type=F32)
    up = jnp.dot(h, wu_ref[...], preferred_element_type=F32)
    act = (_silu(gate) * up).astype(BF16)
    y_ref[0] = x + jnp.dot(act, wd_ref[...], preferred_element_type=F32)


def _ffn(x, o_hgrn, o_na, w_out, norm_w, w_gate, w_up, w_down):
    B, T, _ = x.shape
    tm = TOKEN_TILE
    tok = lambda w: pl.BlockSpec((1, tm, w), lambda b, i: (b, i, 0))
    return pl.pallas_call(
        _ffn_kernel,
        grid=(B, T // tm),
        in_specs=[tok(D_MODEL), tok(HG_W), tok(NA_W), _const_spec(w_out.shape),
                  _const_spec((1, D_MODEL)), _const_spec(w_gate.shape),
                  _const_spec(w_up.shape), _const_spec(w_down.shape)],
        out_specs=tok(D_MODEL),
        out_shape=jax.ShapeDtypeStruct(x.shape, x.dtype),
        compiler_params=pltpu.CompilerParams(
            dimension_semantics=("parallel", "parallel"), vmem_limit_bytes=VMEM_LIMIT_BYTES),
        name="outproj_ffn",
    )(x, o_hgrn, o_na, w_out, norm_w, w_gate, w_up, w_down)


def _encoder_layer(x, p):
    q, ff, fb, v, sg, nq, nk, nv = _inproj(
        x, p["norm_mix_w"], p["w_in"], p["qnw"], p["knw"], p["gmat"])
    o_fwd = _hgrn(q, ff, v, p["lb"], reverse=False)
    o_hgrn = _hgrn(q, fb, v, p["lb"], reverse=True, o_fwd=o_fwd, sg=sg, gnorm_w=p["gnorm_w"])
    o_na = _na(nq, nk, nv, p["na_bias"])
    return _ffn(x, o_hgrn, o_na, p["w_out"], p["norm_ffn_w"], p["w_gate"], p["w_up"], p["w_down"])


def kernel(x_prompt, x_sample, norm_mix_w, w_in, hgrn_lb, hgrn_gnorm_w, na_q_norm_w, na_k_norm_w,
           na_rpb, w_out, norm_ffn_w, w_gate, w_up, w_down):
    head_of_lane = np.arange(NA_W) // NA_DH
    p = {
        "norm_mix_w": norm_mix_w[0][None].astype(F32),
        "w_in": w_in[0].astype(BF16),
        "lb": hgrn_lb.astype(F32),
        "qnw": jnp.tile(na_q_norm_w[0].astype(F32), NA_HEADS)[None],
        "knw": jnp.tile(na_k_norm_w[0].astype(F32), NA_HEADS)[None],
        "gmat": jnp.asarray(head_of_lane[:, None] == head_of_lane[None, :], dtype=BF16),
        "gnorm_w": hgrn_gnorm_w[0][None].astype(F32),
        "na_bias": _na_bias_table(na_rpb[0]),
        "w_out": w_out[0].astype(BF16),
        "norm_ffn_w": norm_ffn_w[0][None].astype(F32),
        "w_gate": w_gate[0].astype(BF16),
        "w_up": w_up[0].astype(BF16),
        "w_down": w_down[0].astype(BF16),
    }
    return (_encoder_layer(x_prompt, p), _encoder_layer(x_sample, p))
```

```python
import functools

import numpy as np
import jax
import jax.numpy as jnp
from jax import lax
from jax.experimental import pallas as pl
from jax.experimental.pallas import tpu as pltpu

F32 = jnp.float32
BF16 = jnp.bfloat16

D_MODEL = 1024
GRID_W = 64
HG_HEADS = 4
HG_D = 128
HG_W = HG_HEADS * HG_D
CHUNK = 64
SUB = 8
FAST_SUB = 16
MAX_BLOCK_DECAY_LOG2 = 80.0
NA_HEADS = 8
NA_DH = 64
NA_W = NA_HEADS * NA_DH
NA_KH = 8
NA_KW = 16
D_FF = 2816
EPS = 1e-6
MASK_VALUE = -1e30

TOKEN_TILE = 512
HGRN_SEQS_PER_STEP = 1
NA_TILE_ROWS = 8
NA_SPAN_ROWS = NA_TILE_ROWS + NA_KH
VMEM_LIMIT_BYTES = 56 * 1024 * 1024


def _silu(x):
    return x * jax.nn.sigmoid(x)


def _const_spec(shape):
    return pl.BlockSpec(shape, lambda *_: (0,) * len(shape), pipeline_mode=pl.Buffered(1))


def _chunk_scan(g, rowmod, reverse):
    n = g.shape[0]
    k = 1
    while k < CHUNK:
        if reverse:
            shifted = pltpu.roll(g, n - k, 0)
            g = g + jnp.where(rowmod < CHUNK - k, shifted, 0.0)
        else:
            shifted = pltpu.roll(g, k, 0)
            g = g + jnp.where(rowmod >= k, shifted, 0.0)
        k *= 2
    return g


N_PROJ_BLOCKS = 8


def _inproj_kernel(x_ref, nw_ref, qnw_ref, knw_ref, gmat_ref, *refs):
    w_refs = refs[:N_PROJ_BLOCKS]
    q_ref, ff_ref, fb_ref, v_ref, sg_ref, nq_ref, nk_ref, nv_ref = refs[N_PROJ_BLOCKS:]
    x = x_ref[0]
    ms = jnp.mean(x * x, axis=-1, keepdims=True)
    h = (x * lax.rsqrt(ms + EPS) * nw_ref[...]).astype(BF16)

    def head_norm(a, w):
        ssum = jnp.dot((a * a).astype(BF16), gmat_ref[...], preferred_element_type=F32)
        return a * lax.rsqrt(ssum * (1.0 / NA_DH) + EPS) * w

    post = [
        (q_ref, lambda a: a),
        (ff_ref, lambda a: a),
        (fb_ref, lambda a: a),
        (v_ref, lambda a: a),
        (sg_ref, _silu),
        (nq_ref, lambda a: head_norm(a, qnw_ref[...]) * (NA_DH ** -0.5)),
        (nk_ref, lambda a: head_norm(a, knw_ref[...])),
        (nv_ref, lambda a: a),
    ]
    for w_ref, (ref, fn) in zip(w_refs, post):
        ref[0] = fn(jnp.dot(h, w_ref[...], preferred_element_type=F32)).astype(ref.dtype)


def _inproj(x, norm_w, w_in, qnw, knw, gmat):
    B, T, _ = x.shape
    tm = TOKEN_TILE
    tok = lambda w: pl.BlockSpec((1, tm, w), lambda b, i: (b, i, 0))
    w_block = lambda c: pl.BlockSpec((D_MODEL, HG_W), lambda b, i: (0, c),
                                     pipeline_mode=pl.Buffered(1))
    f32_out = jax.ShapeDtypeStruct((B, T, HG_W), F32)
    bf16_out = jax.ShapeDtypeStruct((B, T, HG_W), BF16)
    return pl.pallas_call(
        _inproj_kernel,
        grid=(B, T // tm),
        in_specs=[tok(D_MODEL), _const_spec((1, D_MODEL)), _const_spec((1, NA_W)),
                  _const_spec((1, NA_W)), _const_spec((NA_W, NA_W))]
                 + [w_block(c) for c in range(N_PROJ_BLOCKS)],
        out_specs=[tok(HG_W)] * 8,
        out_shape=[f32_out, f32_out, f32_out, bf16_out, f32_out, bf16_out, bf16_out, bf16_out],
        compiler_params=pltpu.CompilerParams(
            dimension_semantics=("parallel", "parallel"), vmem_limit_bytes=VMEM_LIMIT_BYTES),
        name="inproj",
    )(x, norm_w, qnw, knw, gmat, *([w_in] * N_PROJ_BLOCKS))


def _nt_dot(a, b):
    return lax.dot_general(a, b, (((1,), (1,)), ((), ())), preferred_element_type=F32)


def _tn_dot(a, b):
    return lax.dot_general(a, b, (((0,), (0,)), ((), ())), preferred_element_type=F32)


def _hgrn_intra(q_ref, kk_ref, cum_ref, v_ref, c, lanes, reverse, exact):
    def ld(ref, lo, n):
        return ref[c, lo:lo + n, lanes]

    def dif(u, w):
        return (w - u) if reverse else (u - w)

    x_ref, y_ref = (kk_ref, q_ref) if reverse else (q_ref, kk_ref)
    q = ld(q_ref, 0, CHUNK)
    kk = ld(kk_ref, 0, CHUNK)
    cum = ld(cum_ref, 0, CHUNK)
    v = v_ref[c, :, lanes]
    X, Y = (kk, q) if reverse else (q, kk)
    tot = ld(cum_ref, 0 if reverse else CHUNK - 1, 1)
    sub_rows = SUB if exact else FAST_SUB
    nblk = CHUNK // sub_rows
    lane = lax.broadcasted_iota(jnp.int32, (sub_rows, CHUNK), 1)
    sub = lax.broadcasted_iota(jnp.int32, (sub_rows, CHUNK), 0)

    if exact:
        blocks = []
        for I in range(nblk):
            lo = I * SUB
            Xb = X[lo:lo + SUB]
            ab = cum[lo:lo + SUB]
            blk = jnp.zeros((SUB, CHUNK), F32)
            for j in range(SUB):
                yj = ld(y_ref, lo + j, 1)
                aj = ld(cum_ref, lo + j, 1)
                col = jnp.sum(Xb * yj * jnp.exp2(dif(ab, aj)), axis=-1, keepdims=True)
                blk = jnp.where(lane == lo + j, col, blk)
            blocks.append(jnp.where(lane - lo <= sub, blk, 0.0))
        first_row_block = 1
    else:
        blocks = [None] * nblk
        first_row_block = 0

    ncol = nblk - first_row_block
    ends = [ld(cum_ref, J * sub_rows + sub_rows - 1, 1) for J in range(ncol)]
    ye = [Y[J * sub_rows:(J + 1) * sub_rows]
          * jnp.exp2(dif(ends[J], cum[J * sub_rows:(J + 1) * sub_rows])) for J in range(ncol)]
    if ncol < nblk:
        ye.append(jnp.zeros((CHUNK - ncol * sub_rows, HG_D), F32))
    ye = jnp.concatenate(ye, axis=0).astype(BF16)
    starts = [(J + first_row_block) * sub_rows for J in range(ncol)]
    xe = [X[starts[J]:] * jnp.exp2(dif(cum[starts[J]:], ends[J])) for J in range(ncol)]
    res = _nt_dot(jnp.concatenate(xe, axis=0).astype(BF16), ye)
    lane_blk = lane // sub_rows
    off = 0
    for J in range(ncol):
        for I in range(J + first_row_block, nblk):
            r0 = off + I * sub_rows - starts[J]
            piece = res[r0:r0 + sub_rows]
            blocks[I] = piece if blocks[I] is None else jnp.where(lane_blk == J, piece, blocks[I])
        off += CHUNK - starts[J]
    if not exact:
        blocks = [jnp.where(lane - I * sub_rows <= sub, blk, 0.0) for I, blk in enumerate(blocks)]
    L = jnp.concatenate(blocks, axis=0).astype(BF16)

    intra = _tn_dot(L, v) if reverse else jnp.dot(L, v, preferred_element_type=F32)
    qe = (q * jnp.exp2(cum)).astype(BF16)
    ke = (kk * jnp.exp2(tot - cum)).astype(BF16)
    decay = jnp.transpose(jnp.broadcast_to(jnp.exp2(tot), (HG_D, HG_D)))
    return intra, _tn_dot(ke, v), qe, decay


def _gate_lower_bound(lb_ref, reverse):
    d = 1 if reverse else 0
    a0 = lb_ref[0, d:d + 1, :]
    a1 = lb_ref[1, d:d + 1, :]
    m = jnp.maximum(a0, a1)
    e0 = jnp.exp(a0 - m)
    e1 = jnp.exp(a1 - m)
    return e0 / (e0 + e1)


def _hgrn_tile(seqs, lb, emit, reverse, nchunk, exact):
    def chunk_of(ci):
        return (nchunk - 1 - ci) if reverse else ci

    row = lax.broadcasted_iota(jnp.int32, (CHUNK, HG_W), 0)

    def intra(ci):
        c = chunk_of(ci)
        for sq in seqs:
            q_ref, kk_ref, cum_ref = sq["gates"]
            o_buf, u_buf, q_buf, d_buf = sq["slots"]
            f = lb + (1.0 - lb) * jax.nn.sigmoid(sq["fl"][c])
            q_ref[c] = _silu(sq["qr"][c])
            kk_ref[c] = 1.0 - f
            cum_ref[c] = _chunk_scan(jnp.log2(f), row, reverse)
            for h in range(HG_HEADS):
                lanes = slice(h * HG_D, (h + 1) * HG_D)
                o, u, qe, dec = _hgrn_intra(q_ref, kk_ref, cum_ref, sq["v"], c, lanes, reverse, exact)
                o_buf[ci, h] = o
                u_buf[ci, h] = u
                q_buf[ci, h] = qe
                d_buf[ci, h] = dec

    def inter(ci):
        for sq in seqs:
            o_buf, u_buf, q_buf, d_buf = sq["slots"]
            st_ref = sq["st"]
            for h in range(HG_HEADS):
                lanes = slice(h * HG_D, (h + 1) * HG_D)
                st = st_ref[h]
                emit(sq, chunk_of(ci), lanes,
                     o_buf[ci, h] + jnp.dot(q_buf[ci, h], st.astype(BF16), preferred_element_type=F32))
                st_ref[h] = st * d_buf[ci, h] + u_buf[ci, h]

    intra(0)
    for ci in range(nchunk):
        if ci + 1 < nchunk:
            intra(ci + 1)
        inter(ci)


def _gated_norm(out, of_ref, sg_ref, gw_ref, c, lanes):
    o = out + of_ref[c, :, lanes]
    ms = jnp.mean(o * o, axis=-1, keepdims=True)
    return o * lax.rsqrt(ms + EPS) * gw_ref[...] * sg_ref[c, :, lanes]


def _hgrn_kernel(*refs, reverse, tt):
    if reverse:
        qr_ref, fl_ref, v_ref, lb_ref, of_ref, sg_ref, gw_ref, o_ref = refs[:8]
    else:
        qr_ref, fl_ref, v_ref, lb_ref, o_ref = refs[:5]
    st_ref = refs[-8]
    gates = refs[-7:-4]
    slots = refs[-4:]

    @pl.when(pl.program_id(1) == 0)
    def _():
        st_ref[...] = jnp.zeros_like(st_ref)

    seqs = []
    for s in range(HGRN_SEQS_PER_STEP):
        sq = {"qr": qr_ref.at[s], "fl": fl_ref.at[s], "v": v_ref.at[s], "st": st_ref.at[s],
              "gates": [g.at[s] for g in gates], "slots": [b.at[s] for b in slots],
              "out": o_ref.at[s]}
        if reverse:
            sq["of"] = of_ref.at[s]
            sq["sg"] = sg_ref.at[s]
        seqs.append(sq)

    def emit(sq, c, lanes, out):
        if reverse:
            out = _gated_norm(out, sq["of"], sq["sg"], gw_ref, c, lanes)
        sq["out"][c, :, lanes] = out.astype(o_ref.dtype)

    nchunk = tt // CHUNK
    lb = _gate_lower_bound(lb_ref, reverse)
    bounded = jnp.min(lb) >= 2.0 ** (-MAX_BLOCK_DECAY_LOG2 / (FAST_SUB - 1))

    @pl.when(bounded)
    def _():
        _hgrn_tile(seqs, lb, emit, reverse, nchunk, exact=False)

    @pl.when(jnp.logical_not(bounded))
    def _():
        _hgrn_tile(seqs, lb, emit, reverse, nchunk, exact=True)


def _hgrn(q_raw, f_logit, v, lb_param, reverse, o_fwd=None, sg=None, gnorm_w=None):
    B, T, _ = q_raw.shape
    tt = TOKEN_TILE
    nt = T // tt
    ns = HGRN_SEQS_PER_STEP
    assert B % ns == 0
    if reverse:
        tile = lambda b, i: (b, nt - 1 - i, 0, 0)
    else:
        tile = lambda b, i: (b, i, 0, 0)
    chunked = lambda a: a.reshape(B, T // CHUNK, CHUNK, HG_W)
    tok = pl.BlockSpec((ns, tt // CHUNK, CHUNK, HG_W), tile)
    args = [chunked(q_raw), chunked(f_logit), chunked(v), lb_param]
    in_specs = [tok, tok, tok, _const_spec(lb_param.shape)]
    if reverse:
        args += [chunked(o_fwd), chunked(sg), gnorm_w]
        in_specs += [tok, tok, _const_spec((1, HG_D))]
    out = pl.pallas_call(
        functools.partial(_hgrn_kernel, reverse=reverse, tt=tt),
        grid=(B // ns, nt),
        in_specs=in_specs,
        out_specs=tok,
        out_shape=jax.ShapeDtypeStruct((B, T // CHUNK, CHUNK, HG_W), BF16 if reverse else F32),
        scratch_shapes=_hgrn_scratch(ns, tt // CHUNK),
        compiler_params=pltpu.CompilerParams(
            dimension_semantics=("parallel", "arbitrary"), vmem_limit_bytes=VMEM_LIMIT_BYTES),
        name="hgrn_bwd" if reverse else "hgrn_fwd",
    )(*args)
    return out.reshape(B, T, HG_W)


def _hgrn_scratch(ns, nchunk):
    return [pltpu.VMEM((ns, HG_HEADS, HG_D, HG_D), F32)] + 3 * [
            pltpu.VMEM((ns, nchunk, CHUNK, HG_W), F32)] + [
            pltpu.VMEM((ns, nchunk, HG_HEADS, CHUNK, HG_D), F32),
            pltpu.VMEM((ns, nchunk, HG_HEADS, HG_D, HG_D), F32),
            pltpu.VMEM((ns, nchunk, HG_HEADS, CHUNK, HG_D), BF16),
            pltpu.VMEM((ns, nchunk, HG_HEADS, HG_D, HG_D), F32)]


def _na_bias_table(rpb):
    q = np.arange(GRID_W)
    kc = np.arange(GRID_W)
    win_start = np.clip(q - NA_KW // 2, 0, GRID_W - NA_KW)
    valid = (kc[None, :] >= win_start[:, None]) & (kc[None, :] < win_start[:, None] + NA_KW)
    dc = np.clip(kc[None, :] - q[:, None], -(NA_KW - 1), NA_KW - 1) + NA_KW - 1
    n_c = 2 * NA_KW - 1
    onehot = np.zeros((GRID_W, 2, GRID_W, 2, n_c), np.float32)
    for half in range(2):
        onehot[q[:, None], half, kc[None, :], half, dc] = 1.0
    onehot = jnp.asarray(onehot.reshape(GRID_W, 2 * GRID_W, 2 * n_c))
    valid2 = np.tile(valid, (1, 2))
    n_e = 2 * NA_KH - 2
    rows2 = jnp.stack([rpb[:, 0:n_e], rpb[:, 1:n_e + 1]], axis=2).astype(F32)
    rows2 = rows2.reshape(NA_HEADS // 2, 2, n_e, 2 * n_c).transpose(0, 2, 1, 3)
    tab = jnp.einsum("pexz,qyz->pexqy", rows2, onehot, precision=lax.Precision.HIGHEST)
    tab = jnp.where(valid2[None, None, None], tab, MASK_VALUE)
    return tab.reshape(NA_HEADS // 2, n_e, 2 * GRID_W, 2 * GRID_W)


def _na_span_start(i, rows):
    return jnp.clip(i * NA_TILE_ROWS - NA_KH // 2, 0, rows - NA_SPAN_ROWS)


def _na_kernel(q_ref, k_ref, v_ref, bias_ref, o_ref, sbuf0, sbuf1, *, rows):
    i = pl.program_id(1)
    lane = lax.broadcasted_iota(jnp.int32, (GRID_W, 2 * NA_DH), 1)
    first = lane < NA_DH
    npairs = NA_HEADS // 2
    span_start = _na_span_start(i, rows)

    def window(j):
        r = i * NA_TILE_ROWS + j
        start = jnp.clip(r - NA_KH // 2, 0, rows - NA_KH)
        koff = pl.multiple_of((start - span_start) * GRID_W, GRID_W)
        return r - start, pl.ds(koff, NA_KH * GRID_W), pl.ds(pl.multiple_of(j * GRID_W, GRID_W), GRID_W)

    def scores(j, sbuf):
        shift, krows, qrows = window(j)
        for p in range(npairs):
            lanes = slice(p * 2 * NA_DH, (p + 1) * 2 * NA_DH)
            qp = q_ref[0, qrows, lanes]
            zero = jnp.zeros_like(qp)
            qab = jnp.concatenate([jnp.where(first, qp, zero), jnp.where(first, zero, qp)], axis=0)
            s = _nt_dot(qab, k_ref[0, krows, lanes])
            for m in range(NA_KH // 2):
                cols = slice(m * 2 * GRID_W, (m + 1) * 2 * GRID_W)
                sbuf[p, :, cols] = s[:, cols] + bias_ref[p, NA_KH - 1 - shift + 2 * m]

    def attend(j, sbuf):
        _, krows, qrows = window(j)
        for p in range(npairs):
            lanes = slice(p * 2 * NA_DH, (p + 1) * 2 * NA_DH)
            s = sbuf[p]
            mx = jnp.max(s, axis=-1, keepdims=True)
            e = jnp.exp(s - mx)
            den = jnp.sum(e, axis=-1, keepdims=True)
            pv = jnp.dot(e.astype(BF16), v_ref[0, krows, lanes], preferred_element_type=F32) / den
            o_ref[0, qrows, lanes] = jnp.where(first, pv[:GRID_W], pv[GRID_W:]).astype(o_ref.dtype)

    sbufs = (sbuf0, sbuf1)
    scores(0, sbufs[0])
    for j in range(NA_TILE_ROWS):
        if j + 1 < NA_TILE_ROWS:
            scores(j + 1, sbufs[(j + 1) % 2])
        attend(j, sbufs[j % 2])


def _na(nq, nk, nv, bias):
    B, T, _ = nq.shape
    rows = T // GRID_W
    tile_tokens = NA_TILE_ROWS * GRID_W
    nt = rows // NA_TILE_ROWS
    cur = pl.BlockSpec((1, tile_tokens, NA_W), lambda b, i: (b, i, 0))
    span = pl.BlockSpec((pl.Element(1), pl.Element(NA_SPAN_ROWS * GRID_W), pl.Element(NA_W)),
                        lambda b, i: (b, _na_span_start(i, rows) * GRID_W, 0))
    return pl.pallas_call(
        functools.partial(_na_kernel, rows=rows),
        grid=(B, nt),
        in_specs=[cur, span, span, _const_spec(bias.shape)],
        out_specs=cur,
        out_shape=jax.ShapeDtypeStruct((B, T, NA_W), BF16),
        scratch_shapes=[pltpu.VMEM((NA_HEADS // 2, 2 * GRID_W, NA_KH * GRID_W), F32),
                        pltpu.VMEM((NA_HEADS // 2, 2 * GRID_W, NA_KH * GRID_W), F32)],
        compiler_params=pltpu.CompilerParams(
            dimension_semantics=("parallel", "parallel"), vmem_limit_bytes=VMEM_LIMIT_BYTES),
        name="natten",
    )(nq, nk, nv, bias)


def _ffn_kernel(x_ref, oh_ref, on_ref, wo_ref, nw_ref, wg_ref, wu_ref, wd_ref, y_ref):
    x = x_ref[0]
    x = x + jnp.dot(oh_ref[0], wo_ref[:HG_W], preferred_element_type=F32)
    x = x + jnp.dot(on_ref[0], wo_ref[HG_W:], preferred_element_type=F32)
    ms = jnp.mean(x * x, axis=-1, keepdims=True)
    h = (x * lax.rsqrt(ms + EPS) * nw_ref[...]).astype(BF16)
    gate = jnp.dot(h, wg_ref[...], preferred_element_type=F32)
    up = jnp.dot(h, wu_ref[...], preferred_element_type=F32)
    act = (_silu(gate) * up).astype(BF16)
    y_ref[0] = x + jnp.dot(act, wd_ref[...], preferred_element_type=F32)


def _ffn(x, o_hgrn, o_na, w_out, norm_w, w_gate, w_up, w_down):
    B, T, _ = x.shape
    tm = TOKEN_TILE
    tok = lambda w: pl.BlockSpec((1, tm, w), lambda b, i: (b, i, 0))
    return pl.pallas_call(
        _ffn_kernel,
        grid=(B, T // tm),
        in_specs=[tok(D_MODEL), tok(HG_W), tok(NA_W), _const_spec(w_out.shape),
                  _const_spec((1, D_MODEL)), _const_spec(w_gate.shape),
                  _const_spec(w_up.shape), _const_spec(w_down.shape)],
        out_specs=tok(D_MODEL),
        out_shape=jax.ShapeDtypeStruct(x.shape, x.dtype),
        compiler_params=pltpu.CompilerParams(
            dimension_semantics=("parallel", "parallel"), vmem_limit_bytes=VMEM_LIMIT_BYTES),
        name="outproj_ffn",
    )(x, o_hgrn, o_na, w_out, norm_w, w_gate, w_up, w_down)


def _encoder_layer(x, p):
    q, ff, fb, v, sg, nq, nk, nv = _inproj(
        x, p["norm_mix_w"], p["w_in"], p["qnw"], p["knw"], p["gmat"])
    o_fwd = _hgrn(q, ff, v, p["lb"], reverse=False)
    o_hgrn = _hgrn(q, fb, v, p["lb"], reverse=True, o_fwd=o_fwd, sg=sg, gnorm_w=p["gnorm_w"])
    o_na = _na(nq, nk, nv, p["na_bias"])
    return _ffn(x, o_hgrn, o_na, p["w_out"], p["norm_ffn_w"], p["w_gate"], p["w_up"], p["w_down"])


def kernel(x_prompt, x_sample, norm_mix_w, w_in, hgrn_lb, hgrn_gnorm_w, na_q_norm_w, na_k_norm_w,
           na_rpb, w_out, norm_ffn_w, w_gate, w_up, w_down):
    head_of_lane = np.arange(NA_W) // NA_DH
    p = {
        "norm_mix_w": norm_mix_w[0][None].astype(F32),
        "w_in": w_in[0].astype(BF16),
        "lb": hgrn_lb.astype(F32),
        "qnw": jnp.tile(na_q_norm_w[0].astype(F32), NA_HEADS)[None],
        "knw": jnp.tile(na_k_norm_w[0].astype(F32), NA_HEADS)[None],
        "gmat": jnp.asarray(head_of_lane[:, None] == head_of_lane[None, :], dtype=BF16),
        "gnorm_w": hgrn_gnorm_w[0][None].astype(F32),
        "na_bias": _na_bias_table(na_rpb[0]),
        "w_out": w_out[0].astype(BF16),
        "norm_ffn_w": norm_ffn_w[0][None].astype(F32),
        "w_gate": w_gate[0].astype(BF16),
        "w_up": w_up[0].astype(BF16),
        "w_down": w_down[0].astype(BF16),
    }
    return (_encoder_layer(x_prompt, p), _encoder_layer(x_sample, p))
```

```python
import functools

import numpy as np
import jax
import jax.numpy as jnp
from jax import lax
from jax.experimental import pallas as pl
from jax.experimental.pallas import tpu as pltpu

F32 = jnp.float32
BF16 = jnp.bfloat16

D_MODEL = 1024
GRID_W = 64
HG_HEADS = 4
HG_D = 128
HG_W = HG_HEADS * HG_D
CHUNK = 64
SUB = 8
FAST_SUB = 16
MAX_BLOCK_DECAY_LOG2 = 80.0
NA_HEADS = 8
NA_DH = 64
NA_W = NA_HEADS * NA_DH
NA_KH = 8
NA_KW = 16
D_FF = 2816
EPS = 1e-6
MASK_VALUE = -1e30

TOKEN_TILE = 512
INPROJ_TILE = 1024
HGRN_SEQS_PER_STEP = 1
NA_TILE_ROWS = 8
NA_SPAN_ROWS = NA_TILE_ROWS + NA_KH
VMEM_LIMIT_BYTES = 56 * 1024 * 1024


def _silu(x):
    return x * jax.nn.sigmoid(x)


def _const_spec(shape):
    return pl.BlockSpec(shape, lambda *_: (0,) * len(shape), pipeline_mode=pl.Buffered(1))


def _chunk_scan(g, rowmod, reverse):
    n = g.shape[0]
    k = 1
    while k < CHUNK:
        if reverse:
            shifted = pltpu.roll(g, n - k, 0)
            g = g + jnp.where(rowmod < CHUNK - k, shifted, 0.0)
        else:
            shifted = pltpu.roll(g, k, 0)
            g = g + jnp.where(rowmod >= k, shifted, 0.0)
        k *= 2
    return g


N_PROJ_BLOCKS = 8


def _inproj_kernel(x_ref, nw_ref, qnw_ref, knw_ref, gmat_ref, *refs):
    w_refs = refs[:N_PROJ_BLOCKS]
    q_ref, ff_ref, fb_ref, v_ref, sg_ref, nq_ref, nk_ref, nv_ref = refs[N_PROJ_BLOCKS:]
    x = x_ref[0]
    ms = jnp.mean(x * x, axis=-1, keepdims=True)
    h = (x * lax.rsqrt(ms + EPS) * nw_ref[...]).astype(BF16)

    def head_norm(a, w):
        ssum = jnp.dot((a * a).astype(BF16), gmat_ref[...], preferred_element_type=F32)
        return a * lax.rsqrt(ssum * (1.0 / NA_DH) + EPS) * w

    post = [
        (q_ref, lambda a: a),
        (ff_ref, lambda a: a),
        (fb_ref, lambda a: a),
        (v_ref, lambda a: a),
        (sg_ref, _silu),
        (nq_ref, lambda a: head_norm(a, qnw_ref[...]) * (NA_DH ** -0.5)),
        (nk_ref, lambda a: head_norm(a, knw_ref[...])),
        (nv_ref, lambda a: a),
    ]
    for w_ref, (ref, fn) in zip(w_refs, post):
        ref[0] = fn(jnp.dot(h, w_ref[...], preferred_element_type=F32)).astype(ref.dtype)


def _inproj(x, norm_w, w_in, qnw, knw, gmat):
    B, T, _ = x.shape
    tm = INPROJ_TILE
    tok = lambda w: pl.BlockSpec((1, tm, w), lambda b, i: (b, i, 0))
    w_block = lambda c: pl.BlockSpec((D_MODEL, HG_W), lambda b, i: (0, c),
                                     pipeline_mode=pl.Buffered(1))
    f32_out = jax.ShapeDtypeStruct((B, T, HG_W), F32)
    bf16_out = jax.ShapeDtypeStruct((B, T, HG_W), BF16)
    return pl.pallas_call(
        _inproj_kernel,
        grid=(B, T // tm),
        in_specs=[tok(D_MODEL), _const_spec((1, D_MODEL)), _const_spec((1, NA_W)),
                  _const_spec((1, NA_W)), _const_spec((NA_W, NA_W))]
                 + [w_block(c) for c in range(N_PROJ_BLOCKS)],
        out_specs=[tok(HG_W)] * 8,
        out_shape=[f32_out, f32_out, f32_out, bf16_out, f32_out, bf16_out, bf16_out, bf16_out],
        compiler_params=pltpu.CompilerParams(
            dimension_semantics=("parallel", "parallel"), vmem_limit_bytes=VMEM_LIMIT_BYTES),
        name="inproj",
    )(x, norm_w, qnw, knw, gmat, *([w_in] * N_PROJ_BLOCKS))


def _nt_dot(a, b):
    return lax.dot_general(a, b, (((1,), (1,)), ((), ())), preferred_element_type=F32)


def _tn_dot(a, b):
    return lax.dot_general(a, b, (((0,), (0,)), ((), ())), preferred_element_type=F32)


def _hgrn_intra(q_ref, kk_ref, cum_ref, v_ref, c, lanes, reverse, exact):
    def ld(ref, lo, n):
        return ref[c, lo:lo + n, lanes]

    def dif(u, w):
        return (w - u) if reverse else (u - w)

    x_ref, y_ref = (kk_ref, q_ref) if reverse else (q_ref, kk_ref)
    q = ld(q_ref, 0, CHUNK)
    kk = ld(kk_ref, 0, CHUNK)
    cum = ld(cum_ref, 0, CHUNK)
    v = v_ref[c, :, lanes]
    X, Y = (kk, q) if reverse else (q, kk)
    tot = ld(cum_ref, 0 if reverse else CHUNK - 1, 1)
    sub_rows = SUB if exact else FAST_SUB
    nblk = CHUNK // sub_rows
    lane = lax.broadcasted_iota(jnp.int32, (sub_rows, CHUNK), 1)
    sub = lax.broadcasted_iota(jnp.int32, (sub_rows, CHUNK), 0)

    if exact:
        blocks = []
        for I in range(nblk):
            lo = I * SUB
            Xb = X[lo:lo + SUB]
            ab = cum[lo:lo + SUB]
            blk = jnp.zeros((SUB, CHUNK), F32)
            for j in range(SUB):
                yj = ld(y_ref, lo + j, 1)
                aj = ld(cum_ref, lo + j, 1)
                col = jnp.sum(Xb * yj * jnp.exp2(dif(ab, aj)), axis=-1, keepdims=True)
                blk = jnp.where(lane == lo + j, col, blk)
            blocks.append(jnp.where(lane - lo <= sub, blk, 0.0))
        first_row_block = 1
    else:
        blocks = [None] * nblk
        first_row_block = 0

    ncol = nblk - first_row_block
    ends = [ld(cum_ref, J * sub_rows + sub_rows - 1, 1) for J in range(ncol)]
    ye = [Y[J * sub_rows:(J + 1) * sub_rows]
          * jnp.exp2(dif(ends[J], cum[J * sub_rows:(J + 1) * sub_rows])) for J in range(ncol)]
    if ncol < nblk:
        ye.append(jnp.zeros((CHUNK - ncol * sub_rows, HG_D), F32))
    ye = jnp.concatenate(ye, axis=0).astype(BF16)
    starts = [(J + first_row_block) * sub_rows for J in range(ncol)]
    xe = [X[starts[J]:] * jnp.exp2(dif(cum[starts[J]:], ends[J])) for J in range(ncol)]
    res = _nt_dot(jnp.concatenate(xe, axis=0).astype(BF16), ye)
    lane_blk = lane // sub_rows
    off = 0
    for J in range(ncol):
        for I in range(J + first_row_block, nblk):
            r0 = off + I * sub_rows - starts[J]
            piece = res[r0:r0 + sub_rows]
            blocks[I] = piece if blocks[I] is None else jnp.where(lane_blk == J, piece, blocks[I])
        off += CHUNK - starts[J]
    if not exact:
        blocks = [jnp.where(lane - I * sub_rows <= sub, blk, 0.0) for I, blk in enumerate(blocks)]
    L = jnp.concatenate(blocks, axis=0).astype(BF16)

    intra = _tn_dot(L, v) if reverse else jnp.dot(L, v, preferred_element_type=F32)
    qe = (q * jnp.exp2(cum)).astype(BF16)
    ke = (kk * jnp.exp2(tot - cum)).astype(BF16)
    decay = jnp.transpose(jnp.broadcast_to(jnp.exp2(tot), (HG_D, HG_D)))
    return intra, _tn_dot(ke, v), qe, decay


def _gate_lower_bound(lb_ref, reverse):
    d = 1 if reverse else 0
    a0 = lb_ref[0, d:d + 1, :]
    a1 = lb_ref[1, d:d + 1, :]
    m = jnp.maximum(a0, a1)
    e0 = jnp.exp(a0 - m)
    e1 = jnp.exp(a1 - m)
    return e0 / (e0 + e1)


def _hgrn_tile(seqs, lb, emit, reverse, nchunk, exact):
    def chunk_of(ci):
        return (nchunk - 1 - ci) if reverse else ci

    row = lax.broadcasted_iota(jnp.int32, (CHUNK, HG_W), 0)

    def intra(ci):
        c = chunk_of(ci)
        for sq in seqs:
            q_ref, kk_ref, cum_ref = sq["gates"]
            o_buf, u_buf, q_buf, d_buf = sq["slots"]
            f = lb + (1.0 - lb) * jax.nn.sigmoid(sq["fl"][c])
            q_ref[c] = _silu(sq["qr"][c])
            kk_ref[c] = 1.0 - f
            cum_ref[c] = _chunk_scan(jnp.log2(f), row, reverse)
            for h in range(HG_HEADS):
                lanes = slice(h * HG_D, (h + 1) * HG_D)
                o, u, qe, dec = _hgrn_intra(q_ref, kk_ref, cum_ref, sq["v"], c, lanes, reverse, exact)
                o_buf[ci, h] = o
                u_buf[ci, h] = u
                q_buf[ci, h] = qe
                d_buf[ci, h] = dec

    def inter(ci):
        for sq in seqs:
            o_buf, u_buf, q_buf, d_buf = sq["slots"]
            st_ref = sq["st"]
            for h in range(HG_HEADS):
                lanes = slice(h * HG_D, (h + 1) * HG_D)
                st = st_ref[h]
                emit(sq, chunk_of(ci), lanes,
                     o_buf[ci, h] + jnp.dot(q_buf[ci, h], st.astype(BF16), preferred_element_type=F32))
                st_ref[h] = st * d_buf[ci, h] + u_buf[ci, h]

    intra(0)
    for ci in range(nchunk):
        if ci + 1 < nchunk:
            intra(ci + 1)
        inter(ci)


def _gated_norm(out, of_ref, sg_ref, gw_ref, c, lanes):
    o = out + of_ref[c, :, lanes]
    ms = jnp.mean(o * o, axis=-1, keepdims=True)
    return o * lax.rsqrt(ms + EPS) * gw_ref[...] * sg_ref[c, :, lanes]


def _hgrn_kernel(*refs, reverse, tt):
    if reverse:
        qr_ref, fl_ref, v_ref, lb_ref, of_ref, sg_ref, gw_ref, o_ref = refs[:8]
    else:
        qr_ref, fl_ref, v_ref, lb_ref, o_ref = refs[:5]
    st_ref = refs[-8]
    gates = refs[-7:-4]
    slots = refs[-4:]

    @pl.when(pl.program_id(1) == 0)
    def _():
        st_ref[...] = jnp.zeros_like(st_ref)

    seqs = []
    for s in range(HGRN_SEQS_PER_STEP):
        sq = {"qr": qr_ref.at[s], "fl": fl_ref.at[s], "v": v_ref.at[s], "st": st_ref.at[s],
              "gates": [g.at[s] for g in gates], "slots": [b.at[s] for b in slots],
              "out": o_ref.at[s]}
        if reverse:
            sq["of"] = of_ref.at[s]
            sq["sg"] = sg_ref.at[s]
        seqs.append(sq)

    def emit(sq, c, lanes, out):
        if reverse:
            out = _gated_norm(out, sq["of"], sq["sg"], gw_ref, c, lanes)
        sq["out"][c, :, lanes] = out.astype(o_ref.dtype)

    nchunk = tt // CHUNK
    lb = _gate_lower_bound(lb_ref, reverse)
    bounded = jnp.min(lb) >= 2.0 ** (-MAX_BLOCK_DECAY_LOG2 / (FAST_SUB - 1))

    @pl.when(bounded)
    def _():
        _hgrn_tile(seqs, lb, emit, reverse, nchunk, exact=False)

    @pl.when(jnp.logical_not(bounded))
    def _():
        _hgrn_tile(seqs, lb, emit, reverse, nchunk, exact=True)


def _hgrn(q_raw, f_logit, v, lb_param, reverse, o_fwd=None, sg=None, gnorm_w=None):
    B, T, _ = q_raw.shape
    tt = TOKEN_TILE
    nt = T // tt
    ns = HGRN_SEQS_PER_STEP
    assert B % ns == 0
    if reverse:
        tile = lambda b, i: (b, nt - 1 - i, 0, 0)
    else:
        tile = lambda b, i: (b, i, 0, 0)
    chunked = lambda a: a.reshape(B, T // CHUNK, CHUNK, HG_W)
    tok = pl.BlockSpec((ns, tt // CHUNK, CHUNK, HG_W), tile)
    args = [chunked(q_raw), chunked(f_logit), chunked(v), lb_param]
    in_specs = [tok, tok, tok, _const_spec(lb_param.shape)]
    if reverse:
        args += [chunked(o_fwd), chunked(sg), gnorm_w]
        in_specs += [tok, tok, _const_spec((1, HG_D))]
    out = pl.pallas_call(
        functools.partial(_hgrn_kernel, reverse=reverse, tt=tt),
        grid=(B // ns, nt),
        in_specs=in_specs,
        out_specs=tok,
        out_shape=jax.ShapeDtypeStruct((B, T // CHUNK, CHUNK, HG_W), BF16 if reverse else F32),
        scratch_shapes=_hgrn_scratch(ns, tt // CHUNK),
        compiler_params=pltpu.CompilerParams(
            dimension_semantics=("parallel", "arbitrary"), vmem_limit_bytes=VMEM_LIMIT_BYTES),
        name="hgrn_bwd" if reverse else "hgrn_fwd",
    )(*args)
    return out.reshape(B, T, HG_W)


def _hgrn_scratch(ns, nchunk):
    return [pltpu.VMEM((ns, HG_HEADS, HG_D, HG_D), F32)] + 3 * [
            pltpu.VMEM((ns, nchunk, CHUNK, HG_W), F32)] + [
            pltpu.VMEM((ns, nchunk, HG_HEADS, CHUNK, HG_D), F32),
            pltpu.VMEM((ns, nchunk, HG_HEADS, HG_D, HG_D), F32),
            pltpu.VMEM((ns, nchunk, HG_HEADS, CHUNK, HG_D), BF16),
            pltpu.VMEM((ns, nchunk, HG_HEADS, HG_D, HG_D), F32)]


def _na_bias_table(rpb):
    q = np.arange(GRID_W)
    kc = np.arange(GRID_W)
    win_start = np.clip(q - NA_KW // 2, 0, GRID_W - NA_KW)
    valid = (kc[None, :] >= win_start[:, None]) & (kc[None, :] < win_start[:, None] + NA_KW)
    dc = np.clip(kc[None, :] - q[:, None], -(NA_KW - 1), NA_KW - 1) + NA_KW - 1
    n_c = 2 * NA_KW - 1
    onehot = np.zeros((GRID_W, 2, GRID_W, 2, n_c), np.float32)
    for half in range(2):
        onehot[q[:, None], half, kc[None, :], half, dc] = 1.0
    onehot = jnp.asarray(onehot.reshape(GRID_W, 2 * GRID_W, 2 * n_c))
    valid2 = np.tile(valid, (1, 2))
    n_e = 2 * NA_KH - 2
    rows2 = jnp.stack([rpb[:, 0:n_e], rpb[:, 1:n_e + 1]], axis=2).astype(F32)
    rows2 = rows2.reshape(NA_HEADS // 2, 2, n_e, 2 * n_c).transpose(0, 2, 1, 3)
    tab = jnp.einsum("pexz,qyz->pexqy", rows2, onehot, precision=lax.Precision.HIGHEST)
    tab = jnp.where(valid2[None, None, None], tab, MASK_VALUE)
    return tab.reshape(NA_HEADS // 2, n_e, 2 * GRID_W, 2 * GRID_W)


def _na_span_start(i, rows):
    return jnp.clip(i * NA_TILE_ROWS - NA_KH // 2, 0, rows - NA_SPAN_ROWS)


def _na_kernel(q_ref, k_ref, v_ref, bias_ref, o_ref, sbuf0, sbuf1, *, rows):
    i = pl.program_id(1)
    lane = lax.broadcasted_iota(jnp.int32, (GRID_W, 2 * NA_DH), 1)
    first = lane < NA_DH
    npairs = NA_HEADS // 2
    span_start = _na_span_start(i, rows)

    def window(j):
        r = i * NA_TILE_ROWS + j
        start = jnp.clip(r - NA_KH // 2, 0, rows - NA_KH)
        koff = pl.multiple_of((start - span_start) * GRID_W, GRID_W)
        return r - start, pl.ds(koff, NA_KH * GRID_W), pl.ds(pl.multiple_of(j * GRID_W, GRID_W), GRID_W)

    def scores(j, sbuf):
        shift, krows, qrows = window(j)
        for p in range(npairs):
            lanes = slice(p * 2 * NA_DH, (p + 1) * 2 * NA_DH)
            qp = q_ref[0, qrows, lanes]
            zero = jnp.zeros_like(qp)
            qab = jnp.concatenate([jnp.where(first, qp, zero), jnp.where(first, zero, qp)], axis=0)
            s = _nt_dot(qab, k_ref[0, krows, lanes])
            for m in range(NA_KH // 2):
                cols = slice(m * 2 * GRID_W, (m + 1) * 2 * GRID_W)
                sbuf[p, :, cols] = s[:, cols] + bias_ref[p, NA_KH - 1 - shift + 2 * m]

    def attend(j, sbuf):
        _, krows, qrows = window(j)
        for p in range(npairs):
            lanes = slice(p * 2 * NA_DH, (p + 1) * 2 * NA_DH)
            s = sbuf[p]
            mx = jnp.max(s, axis=-1, keepdims=True)
            e = jnp.exp(s - mx)
            den = jnp.sum(e, axis=-1, keepdims=True)
            pv = jnp.dot(e.astype(BF16), v_ref[0, krows, lanes], preferred_element_type=F32) / den
            o_ref[0, qrows, lanes] = jnp.where(first, pv[:GRID_W], pv[GRID_W:]).astype(o_ref.dtype)

    sbufs = (sbuf0, sbuf1)
    scores(0, sbufs[0])
    for j in range(NA_TILE_ROWS):
        if j + 1 < NA_TILE_ROWS:
            scores(j + 1, sbufs[(j + 1) % 2])
        attend(j, sbufs[j % 2])


def _na(nq, nk, nv, bias):
    B, T, _ = nq.shape
    rows = T // GRID_W
    tile_tokens = NA_TILE_ROWS * GRID_W
    nt = rows // NA_TILE_ROWS
    cur = pl.BlockSpec((1, tile_tokens, NA_W), lambda b, i: (b, i, 0))
    span = pl.BlockSpec((pl.Element(1), pl.Element(NA_SPAN_ROWS * GRID_W), pl.Element(NA_W)),
                        lambda b, i: (b, _na_span_start(i, rows) * GRID_W, 0))
    return pl.pallas_call(
        functools.partial(_na_kernel, rows=rows),
        grid=(B, nt),
        in_specs=[cur, span, span, _const_spec(bias.shape)],
        out_specs=cur,
        out_shape=jax.ShapeDtypeStruct((B, T, NA_W), BF16),
        scratch_shapes=[pltpu.VMEM((NA_HEADS // 2, 2 * GRID_W, NA_KH * GRID_W), F32),
                        pltpu.VMEM((NA_HEADS // 2, 2 * GRID_W, NA_KH * GRID_W), F32)],
        compiler_params=pltpu.CompilerParams(
            dimension_semantics=("parallel", "parallel"), vmem_limit_bytes=VMEM_LIMIT_BYTES),
        name="natten",
    )(nq, nk, nv, bias)


def _ffn_kernel(x_ref, oh_ref, on_ref, wo_ref, nw_ref, wg_ref, wu_ref, wd_ref, y_ref):
    halves = [slice(0, TOKEN_TILE // 2), slice(TOKEN_TILE // 2, TOKEN_TILE)]
    xs = [x_ref[0, r] + jnp.dot(oh_ref[0, r], wo_ref[:HG_W], preferred_element_type=F32)
          + jnp.dot(on_ref[0, r], wo_ref[HG_W:], preferred_element_type=F32) for r in halves]
    hs = [(x * lax.rsqrt(jnp.mean(x * x, axis=-1, keepdims=True) + EPS) * nw_ref[...]).astype(BF16)
          for x in xs]
    gates = [jnp.dot(h, wg_ref[...], preferred_element_type=F32) for h in hs]
    ups = [jnp.dot(h, wu_ref[...], preferred_element_type=F32) for h in hs]
    acts = [(_silu(g) * u).astype(BF16) for g, u in zip(gates, ups)]
    for r, x, act in zip(halves, xs, acts):
        y_ref[0, r] = x + jnp.dot(act, wd_ref[...], preferred_element_type=F32)


def _ffn(x, o_hgrn, o_na, w_out, norm_w, w_gate, w_up, w_down):
    B, T, _ = x.shape
    tm = TOKEN_TILE
    tok = lambda w: pl.BlockSpec((1, tm, w), lambda b, i: (b, i, 0))
    return pl.pallas_call(
        _ffn_kernel,
        grid=(B, T // tm),
        in_specs=[tok(D_MODEL), tok(HG_W), tok(NA_W), _const_spec(w_out.shape),
                  _const_spec((1, D_MODEL)), _const_spec(w_gate.shape),
                  _const_spec(w_up.shape), _const_spec(w_down.shape)],
        out_specs=tok(D_MODEL),
        out_shape=jax.ShapeDtypeStruct(x.shape, x.dtype),
        compiler_params=pltpu.CompilerParams(
            dimension_semantics=("parallel", "parallel"), vmem_limit_bytes=VMEM_LIMIT_BYTES),
        name="outproj_ffn",
    )(x, o_hgrn, o_na, w_out, norm_w, w_gate, w_up, w_down)


def _encoder_layer(x, p):
    q, ff, fb, v, sg, nq, nk, nv = _inproj(
        x, p["norm_mix_w"], p["w_in"], p["qnw"], p["knw"], p["gmat"])
    o_fwd = _hgrn(q, ff, v, p["lb"], reverse=False)
    o_hgrn = _hgrn(q, fb, v, p["lb"], reverse=True, o_fwd=o_fwd, sg=sg, gnorm_w=p["gnorm_w"])
    o_na = _na(nq, nk, nv, p["na_bias"])
    return _ffn(x, o_hgrn, o_na, p["w_out"], p["norm_ffn_w"], p["w_gate"], p["w_up"], p["w_down"])


def kernel(x_prompt, x_sample, norm_mix_w, w_in, hgrn_lb, hgrn_gnorm_w, na_q_norm_w, na_k_norm_w,
           na_rpb, w_out, norm_ffn_w, w_gate, w_up, w_down):
    head_of_lane = np.arange(NA_W) // NA_DH
    p = {
        "norm_mix_w": norm_mix_w[0][None].astype(F32),
        "w_in": w_in[0].astype(BF16),
        "lb": hgrn_lb.astype(F32),
        "qnw": jnp.tile(na_q_norm_w[0].astype(F32), NA_HEADS)[None],
        "knw": jnp.tile(na_k_norm_w[0].astype(F32), NA_HEADS)[None],
        "gmat": jnp.asarray(head_of_lane[:, None] == head_of_lane[None, :], dtype=BF16),
        "gnorm_w": hgrn_gnorm_w[0][None].astype(F32),
        "na_bias": _na_bias_table(na_rpb[0]),
        "w_out": w_out[0].astype(BF16),
        "norm_ffn_w": norm_ffn_w[0][None].astype(F32),
        "w_gate": w_gate[0].astype(BF16),
        "w_up": w_up[0].astype(BF16),
        "w_down": w_down[0].astype(BF16),
    }
    return (_encoder_layer(x_prompt, p), _encoder_layer(x_sample, p))
```

```python
import functools

import numpy as np
import jax
import jax.numpy as jnp
from jax import lax
from jax.experimental import pallas as pl
from jax.experimental.pallas import tpu as pltpu

F32 = jnp.float32
BF16 = jnp.bfloat16

D_MODEL = 1024
GRID_W = 64
HG_HEADS = 4
HG_D = 128
HG_W = HG_HEADS * HG_D
CHUNK = 64
SUB = 8
FAST_SUB = 16
MAX_BLOCK_DECAY_LOG2 = 80.0
NA_HEADS = 8
NA_DH = 64
NA_W = NA_HEADS * NA_DH
NA_KH = 8
NA_KW = 16
D_FF = 2816
EPS = 1e-6
MASK_VALUE = -1e30

TOKEN_TILE = 512
HGRN_TILE = 1024
INPROJ_TILE = 1024
HGRN_SEQS_PER_STEP = 1
NA_TILE_ROWS = 16
NA_SPAN_ROWS = NA_TILE_ROWS + NA_KH
VMEM_LIMIT_BYTES = 56 * 1024 * 1024


def _sigmoid(x):
    return 0.5 * jnp.tanh(0.5 * x) + 0.5


def _silu(x):
    return x * _sigmoid(x)


def _const_spec(shape):
    return pl.BlockSpec(shape, lambda *_: (0,) * len(shape), pipeline_mode=pl.Buffered(1))


def _chunk_scan(g, rowmod, reverse):
    n = g.shape[0]
    k = 1
    while k < CHUNK:
        if reverse:
            shifted = pltpu.roll(g, n - k, 0)
            g = g + jnp.where(rowmod < CHUNK - k, shifted, 0.0)
        else:
            shifted = pltpu.roll(g, k, 0)
            g = g + jnp.where(rowmod >= k, shifted, 0.0)
        k *= 2
    return g


N_PROJ_BLOCKS = 8


def _inproj_kernel(x_ref, nw_ref, qnw_ref, knw_ref, gmat_ref, *refs):
    w_refs = refs[:N_PROJ_BLOCKS]
    q_ref, ff_ref, fb_ref, v_ref, sg_ref, nq_ref, nk_ref, nv_ref = refs[N_PROJ_BLOCKS:]
    x = x_ref[0]
    ms = jnp.mean(x * x, axis=-1, keepdims=True)
    h = (x * lax.rsqrt(ms + EPS) * nw_ref[...]).astype(BF16)

    def head_norm(a, w):
        ssum = jnp.dot((a * a).astype(BF16), gmat_ref[...], preferred_element_type=F32)
        return a * lax.rsqrt(ssum * (1.0 / NA_DH) + EPS) * w

    post = [
        (q_ref, lambda a: a),
        (ff_ref, lambda a: a),
        (fb_ref, lambda a: a),
        (v_ref, lambda a: a),
        (sg_ref, _silu),
        (nq_ref, lambda a: head_norm(a, qnw_ref[...]) * (NA_DH ** -0.5)),
        (nk_ref, lambda a: head_norm(a, knw_ref[...])),
        (nv_ref, lambda a: a),
    ]
    for w_ref, (ref, fn) in zip(w_refs, post):
        ref[0] = fn(jnp.dot(h, w_ref[...], preferred_element_type=F32)).astype(ref.dtype)


def _inproj(x, norm_w, w_in, qnw, knw, gmat):
    B, T, _ = x.shape
    tm = INPROJ_TILE
    tok = lambda w: pl.BlockSpec((1, tm, w), lambda b, i: (b, i, 0))
    w_block = lambda c: pl.BlockSpec((D_MODEL, HG_W), lambda b, i: (0, c),
                                     pipeline_mode=pl.Buffered(1))
    f32_out = jax.ShapeDtypeStruct((B, T, HG_W), F32)
    bf16_out = jax.ShapeDtypeStruct((B, T, HG_W), BF16)
    return pl.pallas_call(
        _inproj_kernel,
        grid=(B, T // tm),
        in_specs=[tok(D_MODEL), _const_spec((1, D_MODEL)), _const_spec((1, NA_W)),
                  _const_spec((1, NA_W)), _const_spec((NA_W, NA_W))]
                 + [w_block(c) for c in range(N_PROJ_BLOCKS)],
        out_specs=[tok(HG_W)] * 8,
        out_shape=[f32_out, f32_out, f32_out, bf16_out, f32_out, bf16_out, bf16_out, bf16_out],
        compiler_params=pltpu.CompilerParams(
            dimension_semantics=("parallel", "parallel"), vmem_limit_bytes=VMEM_LIMIT_BYTES),
        name="inproj",
    )(x, norm_w, qnw, knw, gmat, *([w_in] * N_PROJ_BLOCKS))


def _nt_dot(a, b):
    return lax.dot_general(a, b, (((1,), (1,)), ((), ())), preferred_element_type=F32)


def _tn_dot(a, b):
    return lax.dot_general(a, b, (((0,), (0,)), ((), ())), preferred_element_type=F32)


def _hgrn_intra(q_ref, kk_ref, cum_ref, v_ref, c, lanes, reverse, exact):
    def ld(ref, lo, n):
        return ref[c, lo:lo + n, lanes]

    def dif(u, w):
        return (w - u) if reverse else (u - w)

    x_ref, y_ref = (kk_ref, q_ref) if reverse else (q_ref, kk_ref)
    q = ld(q_ref, 0, CHUNK)
    kk = ld(kk_ref, 0, CHUNK)
    cum = ld(cum_ref, 0, CHUNK)
    v = v_ref[c, :, lanes]
    X, Y = (kk, q) if reverse else (q, kk)
    tot = ld(cum_ref, 0 if reverse else CHUNK - 1, 1)
    sub_rows = SUB if exact else FAST_SUB
    nblk = CHUNK // sub_rows
    lane = lax.broadcasted_iota(jnp.int32, (sub_rows, CHUNK), 1)
    sub = lax.broadcasted_iota(jnp.int32, (sub_rows, CHUNK), 0)

    if exact:
        blocks = []
        for I in range(nblk):
            lo = I * SUB
            Xb = X[lo:lo + SUB]
            ab = cum[lo:lo + SUB]
            blk = jnp.zeros((SUB, CHUNK), F32)
            for j in range(SUB):
                yj = ld(y_ref, lo + j, 1)
                aj = ld(cum_ref, lo + j, 1)
                col = jnp.sum(Xb * yj * jnp.exp2(dif(ab, aj)), axis=-1, keepdims=True)
                blk = jnp.where(lane == lo + j, col, blk)
            blocks.append(jnp.where(lane - lo <= sub, blk, 0.0))
        first_row_block = 1
    else:
        blocks = [None] * nblk
        first_row_block = 0

    ncol = nblk - first_row_block
    ends = [ld(cum_ref, J * sub_rows + sub_rows - 1, 1) for J in range(ncol)]
    ye = [Y[J * sub_rows:(J + 1) * sub_rows]
          * jnp.exp2(dif(ends[J], cum[J * sub_rows:(J + 1) * sub_rows])) for J in range(ncol)]
    if ncol < nblk:
        ye.append(jnp.zeros((CHUNK - ncol * sub_rows, HG_D), F32))
    ye = jnp.concatenate(ye, axis=0).astype(BF16)
    starts = [(J + first_row_block) * sub_rows for J in range(ncol)]
    xe = [X[starts[J]:] * jnp.exp2(dif(cum[starts[J]:], ends[J])) for J in range(ncol)]
    res = _nt_dot(jnp.concatenate(xe, axis=0).astype(BF16), ye)
    lane_blk = lane // sub_rows
    off = 0
    for J in range(ncol):
        for I in range(J + first_row_block, nblk):
            r0 = off + I * sub_rows - starts[J]
            piece = res[r0:r0 + sub_rows]
            blocks[I] = piece if blocks[I] is None else jnp.where(lane_blk == J, piece, blocks[I])
        off += CHUNK - starts[J]
    if not exact:
        blocks = [jnp.where(lane - I * sub_rows <= sub, blk, 0.0) for I, blk in enumerate(blocks)]
    L = jnp.concatenate(blocks, axis=0).astype(BF16)

    intra = _tn_dot(L, v) if reverse else jnp.dot(L, v, preferred_element_type=F32)
    qe = (q * jnp.exp2(cum)).astype(BF16)
    ke = (kk * jnp.exp2(tot - cum)).astype(BF16)
    decay = jnp.transpose(jnp.broadcast_to(jnp.exp2(tot), (HG_D, HG_D)))
    return intra, _tn_dot(ke, v), qe, decay


def _gate_lower_bound(lb_ref, reverse):
    d = 1 if reverse else 0
    a0 = lb_ref[0, d:d + 1, :]
    a1 = lb_ref[1, d:d + 1, :]
    m = jnp.maximum(a0, a1)
    e0 = jnp.exp(a0 - m)
    e1 = jnp.exp(a1 - m)
    return e0 / (e0 + e1)


def _hgrn_tile(seqs, lb, emit, reverse, nchunk, exact):
    def chunk_of(ci):
        return (nchunk - 1 - ci) if reverse else ci

    row = lax.broadcasted_iota(jnp.int32, (CHUNK, HG_W), 0)

    def intra(ci):
        c = chunk_of(ci)
        for sq in seqs:
            q_ref, kk_ref, cum_ref = sq["gates"]
            o_buf, u_buf, q_buf, d_buf = sq["slots"]
            f = lb + (1.0 - lb) * _sigmoid(sq["fl"][c])
            q_ref[c] = _silu(sq["qr"][c])
            kk_ref[c] = 1.0 - f
            cum_ref[c] = _chunk_scan(jnp.log2(f), row, reverse)
            for h in range(HG_HEADS):
                lanes = slice(h * HG_D, (h + 1) * HG_D)
                o, u, qe, dec = _hgrn_intra(q_ref, kk_ref, cum_ref, sq["v"], c, lanes, reverse, exact)
                o_buf[ci, h] = o
                u_buf[ci, h] = u
                q_buf[ci, h] = qe
                d_buf[ci, h] = dec

    def inter(ci):
        for sq in seqs:
            o_buf, u_buf, q_buf, d_buf = sq["slots"]
            st_ref = sq["st"]
            for h in range(HG_HEADS):
                lanes = slice(h * HG_D, (h + 1) * HG_D)
                st = st_ref[h]
                emit(sq, chunk_of(ci), lanes,
                     o_buf[ci, h] + jnp.dot(q_buf[ci, h], st.astype(BF16), preferred_element_type=F32))
                st_ref[h] = st * d_buf[ci, h] + u_buf[ci, h]

    intra(0)
    for ci in range(nchunk):
        if ci + 1 < nchunk:
            intra(ci + 1)
        inter(ci)


def _gated_norm(out, of_ref, sg_ref, gw_ref, c, lanes):
    o = out + of_ref[c, :, lanes]
    ms = jnp.mean(o * o, axis=-1, keepdims=True)
    return o * lax.rsqrt(ms + EPS) * gw_ref[...] * sg_ref[c, :, lanes]


def _hgrn_kernel(*refs, reverse, tt):
    if reverse:
        qr_ref, fl_ref, v_ref, lb_ref, of_ref, sg_ref, gw_ref, o_ref = refs[:8]
    else:
        qr_ref, fl_ref, v_ref, lb_ref, o_ref = refs[:5]
    st_ref = refs[-8]
    gates = refs[-7:-4]
    slots = refs[-4:]

    @pl.when(pl.program_id(1) == 0)
    def _():
        st_ref[...] = jnp.zeros_like(st_ref)

    seqs = []
    for s in range(HGRN_SEQS_PER_STEP):
        sq = {"qr": qr_ref.at[s], "fl": fl_ref.at[s], "v": v_ref.at[s], "st": st_ref.at[s],
              "gates": [g.at[s] for g in gates], "slots": [b.at[s] for b in slots],
              "out": o_ref.at[s]}
        if reverse:
            sq["of"] = of_ref.at[s]
            sq["sg"] = sg_ref.at[s]
        seqs.append(sq)

    def emit(sq, c, lanes, out):
        if reverse:
            out = _gated_norm(out, sq["of"], sq["sg"], gw_ref, c, lanes)
        sq["out"][c, :, lanes] = out.astype(o_ref.dtype)

    nchunk = tt // CHUNK
    lb = _gate_lower_bound(lb_ref, reverse)
    bounded = jnp.min(lb) >= 2.0 ** (-MAX_BLOCK_DECAY_LOG2 / (FAST_SUB - 1))

    @pl.when(bounded)
    def _():
        _hgrn_tile(seqs, lb, emit, reverse, nchunk, exact=False)

    @pl.when(jnp.logical_not(bounded))
    def _():
        _hgrn_tile(seqs, lb, emit, reverse, nchunk, exact=True)


def _hgrn(q_raw, f_logit, v, lb_param, reverse, o_fwd=None, sg=None, gnorm_w=None):
    B, T, _ = q_raw.shape
    tt = HGRN_TILE
    nt = T // tt
    ns = HGRN_SEQS_PER_STEP
    assert B % ns == 0
    if reverse:
        tile = lambda b, i: (b, nt - 1 - i, 0, 0)
    else:
        tile = lambda b, i: (b, i, 0, 0)
    chunked = lambda a: a.reshape(B, T // CHUNK, CHUNK, HG_W)
    tok = pl.BlockSpec((ns, tt // CHUNK, CHUNK, HG_W), tile)
    args = [chunked(q_raw), chunked(f_logit), chunked(v), lb_param]
    in_specs = [tok, tok, tok, _const_spec(lb_param.shape)]
    if reverse:
        args += [chunked(o_fwd), chunked(sg), gnorm_w]
        in_specs += [tok, tok, _const_spec((1, HG_D))]
    out = pl.pallas_call(
        functools.partial(_hgrn_kernel, reverse=reverse, tt=tt),
        grid=(B // ns, nt),
        in_specs=in_specs,
        out_specs=tok,
        out_shape=jax.ShapeDtypeStruct((B, T // CHUNK, CHUNK, HG_W), BF16 if reverse else F32),
        scratch_shapes=_hgrn_scratch(ns, tt // CHUNK),
        compiler_params=pltpu.CompilerParams(
            dimension_semantics=("parallel", "arbitrary"), vmem_limit_bytes=VMEM_LIMIT_BYTES),
        name="hgrn_bwd" if reverse else "hgrn_fwd",
    )(*args)
    return out.reshape(B, T, HG_W)


def _hgrn_scratch(ns, nchunk):
    return [pltpu.VMEM((ns, HG_HEADS, HG_D, HG_D), F32)] + 3 * [
            pltpu.VMEM((ns, nchunk, CHUNK, HG_W), F32)] + [
            pltpu.VMEM((ns, nchunk, HG_HEADS, CHUNK, HG_D), F32),
            pltpu.VMEM((ns, nchunk, HG_HEADS, HG_D, HG_D), F32),
            pltpu.VMEM((ns, nchunk, HG_HEADS, CHUNK, HG_D), BF16),
            pltpu.VMEM((ns, nchunk, HG_HEADS, HG_D, HG_D), F32)]


def _na_bias_table(rpb):
    q = np.arange(GRID_W)
    kc = np.arange(GRID_W)
    win_start = np.clip(q - NA_KW // 2, 0, GRID_W - NA_KW)
    valid = (kc[None, :] >= win_start[:, None]) & (kc[None, :] < win_start[:, None] + NA_KW)
    dc = np.clip(kc[None, :] - q[:, None], -(NA_KW - 1), NA_KW - 1) + NA_KW - 1
    n_c = 2 * NA_KW - 1
    onehot = np.zeros((GRID_W, 2, GRID_W, 2, n_c), np.float32)
    for half in range(2):
        onehot[q[:, None], half, kc[None, :], half, dc] = 1.0
    onehot = jnp.asarray(onehot.reshape(GRID_W, 2 * GRID_W, 2 * n_c))
    valid2 = np.tile(valid, (1, 2))
    n_e = 2 * NA_KH - 2
    rows2 = jnp.stack([rpb[:, 0:n_e], rpb[:, 1:n_e + 1]], axis=2).astype(F32)
    rows2 = rows2.reshape(NA_HEADS // 2, 2, n_e, 2 * n_c).transpose(0, 2, 1, 3)
    tab = jnp.einsum("pexz,qyz->pexqy", rows2, onehot, precision=lax.Precision.HIGHEST)
    tab = jnp.where(valid2[None, None, None], tab, MASK_VALUE)
    return tab.reshape(NA_HEADS // 2, n_e, 2 * GRID_W, 2 * GRID_W)


def _na_span_start(i, rows):
    return jnp.clip(i * NA_TILE_ROWS - NA_KH // 2, 0, rows - NA_SPAN_ROWS)


def _na_kernel(q_ref, k_ref, v_ref, bias_ref, o_ref, sbuf0, sbuf1, *, rows):
    i = pl.program_id(1)
    lane = lax.broadcasted_iota(jnp.int32, (GRID_W, 2 * NA_DH), 1)
    first = lane < NA_DH
    npairs = NA_HEADS // 2
    span_start = _na_span_start(i, rows)

    def window(j):
        r = i * NA_TILE_ROWS + j
        start = jnp.clip(r - NA_KH // 2, 0, rows - NA_KH)
        koff = pl.multiple_of((start - span_start) * GRID_W, GRID_W)
        return r - start, pl.ds(koff, NA_KH * GRID_W), pl.ds(pl.multiple_of(j * GRID_W, GRID_W), GRID_W)

    def scores(j, sbuf):
        shift, krows, qrows = window(j)
        for p in range(npairs):
            lanes = slice(p * 2 * NA_DH, (p + 1) * 2 * NA_DH)
            qp = q_ref[0, qrows, lanes]
            zero = jnp.zeros_like(qp)
            qab = jnp.concatenate([jnp.where(first, qp, zero), jnp.where(first, zero, qp)], axis=0)
            s = _nt_dot(qab, k_ref[0, krows, lanes])
            for m in range(NA_KH // 2):
                cols = slice(m * 2 * GRID_W, (m + 1) * 2 * GRID_W)
                sbuf[p, :, cols] = s[:, cols] + bias_ref[p, NA_KH - 1 - shift + 2 * m]

    def attend(j, sbuf):
        _, krows, qrows = window(j)
        for p in range(npairs):
            lanes = slice(p * 2 * NA_DH, (p + 1) * 2 * NA_DH)
            s = sbuf[p]
            mx = jnp.max(s, axis=-1, keepdims=True)
            e = jnp.exp(s - mx)
            den = jnp.sum(e, axis=-1, keepdims=True)
            pv = jnp.dot(e.astype(BF16), v_ref[0, krows, lanes], preferred_element_type=F32) / den
            o_ref[0, qrows, lanes] = jnp.where(first, pv[:GRID_W], pv[GRID_W:]).astype(o_ref.dtype)

    sbufs = (sbuf0, sbuf1)
    scores(0, sbufs[0])
    for j in range(NA_TILE_ROWS):
        if j + 1 < NA_TILE_ROWS:
            scores(j + 1, sbufs[(j + 1) % 2])
        attend(j, sbufs[j % 2])


def _na(nq, nk, nv, bias):
    B, T, _ = nq.shape
    rows = T // GRID_W
    tile_tokens = NA_TILE_ROWS * GRID_W
    nt = rows // NA_TILE_ROWS
    cur = pl.BlockSpec((1, tile_tokens, NA_W), lambda b, i: (b, i, 0))
    span = pl.BlockSpec((pl.Element(1), pl.Element(NA_SPAN_ROWS * GRID_W), pl.Element(NA_W)),
                        lambda b, i: (b, _na_span_start(i, rows) * GRID_W, 0))
    return pl.pallas_call(
        functools.partial(_na_kernel, rows=rows),
        grid=(B, nt),
        in_specs=[cur, span, span, _const_spec(bias.shape)],
        out_specs=cur,
        out_shape=jax.ShapeDtypeStruct((B, T, NA_W), BF16),
        scratch_shapes=[pltpu.VMEM((NA_HEADS // 2, 2 * GRID_W, NA_KH * GRID_W), F32),
                        pltpu.VMEM((NA_HEADS // 2, 2 * GRID_W, NA_KH * GRID_W), F32)],
        compiler_params=pltpu.CompilerParams(
            dimension_semantics=("parallel", "parallel"), vmem_limit_bytes=VMEM_LIMIT_BYTES),
        name="natten",
    )(nq, nk, nv, bias)


def _ffn_kernel(x_ref, oh_ref, on_ref, wo_ref, nw_ref, wg_ref, wu_ref, wd_ref, y_ref):
    halves = [slice(0, TOKEN_TILE // 2), slice(TOKEN_TILE // 2, TOKEN_TILE)]
    xs = [x_ref[0, r] + jnp.dot(oh_ref[0, r], wo_ref[:HG_W], preferred_element_type=F32)
          + jnp.dot(on_ref[0, r], wo_ref[HG_W:], preferred_element_type=F32) for r in halves]
    hs = [(x * lax.rsqrt(jnp.mean(x * x, axis=-1, keepdims=True) + EPS) * nw_ref[...]).astype(BF16)
          for x in xs]
    gates = [jnp.dot(h, wg_ref[...], preferred_element_type=F32) for h in hs]
    ups = [jnp.dot(h, wu_ref[...], preferred_element_type=F32) for h in hs]
    acts = [(_silu(g) * u).astype(BF16) for g, u in zip(gates, ups)]
    for r, x, act in zip(halves, xs, acts):
        y_ref[0, r] = x + jnp.dot(act, wd_ref[...], preferred_element_type=F32)


def _ffn(x, o_hgrn, o_na, w_out, norm_w, w_gate, w_up, w_down):
    B, T, _ = x.shape
    tm = TOKEN_TILE
    tok = lambda w: pl.BlockSpec((1, tm, w), lambda b, i: (b, i, 0))
    return pl.pallas_call(
        _ffn_kernel,
        grid=(B, T // tm),
        in_specs=[tok(D_MODEL), tok(HG_W), tok(NA_W), _const_spec(w_out.shape),
                  _const_spec((1, D_MODEL)), _const_spec(w_gate.shape),
                  _const_spec(w_up.shape), _const_spec(w_down.shape)],
        out_specs=tok(D_MODEL),
        out_shape=jax.ShapeDtypeStruct(x.shape, x.dtype),
        compiler_params=pltpu.CompilerParams(
            dimension_semantics=("parallel", "parallel"), vmem_limit_bytes=VMEM_LIMIT_BYTES),
        name="outproj_ffn",
    )(x, o_hgrn, o_na, w_out, norm_w, w_gate, w_up, w_down)


def _encoder_layer(x, p):
    q, ff, fb, v, sg, nq, nk, nv = _inproj(
        x, p["norm_mix_w"], p["w_in"], p["qnw"], p["knw"], p["gmat"])
    o_fwd = _hgrn(q, ff, v, p["lb"], reverse=False)
    o_hgrn = _hgrn(q, fb, v, p["lb"], reverse=True, o_fwd=o_fwd, sg=sg, gnorm_w=p["gnorm_w"])
    o_na = _na(nq, nk, nv, p["na_bias"])
    return _ffn(x, o_hgrn, o_na, p["w_out"], p["norm_ffn_w"], p["w_gate"], p["w_up"], p["w_down"])


def kernel(x_prompt, x_sample, norm_mix_w, w_in, hgrn_lb, hgrn_gnorm_w, na_q_norm_w, na_k_norm_w,
           na_rpb, w_out, norm_ffn_w, w_gate, w_up, w_down):
    head_of_lane = np.arange(NA_W) // NA_DH
    p = {
        "norm_mix_w": norm_mix_w[0][None].astype(F32),
        "w_in": w_in[0].astype(BF16),
        "lb": hgrn_lb.astype(F32),
        "qnw": jnp.tile(na_q_norm_w[0].astype(F32), NA_HEADS)[None],
        "knw": jnp.tile(na_k_norm_w[0].astype(F32), NA_HEADS)[None],
        "gmat": jnp.asarray(head_of_lane[:, None] == head_of_lane[None, :], dtype=BF16),
        "gnorm_w": hgrn_gnorm_w[0][None].astype(F32),
        "na_bias": _na_bias_table(na_rpb[0]),
        "w_out": w_out[0].astype(BF16),
        "norm_ffn_w": norm_ffn_w[0][None].astype(F32),
        "w_gate": w_gate[0].astype(BF16),
        "w_up": w_up[0].astype(BF16),
        "w_down": w_down[0].astype(BF16),
    }
    return (_encoder_layer(x_prompt, p), _encoder_layer(x_sample, p))
```

```python
import functools

import numpy as np
import jax
import jax.numpy as jnp
from jax import lax
from jax.experimental import pallas as pl
from jax.experimental.pallas import tpu as pltpu

F32 = jnp.float32
BF16 = jnp.bfloat16

D_MODEL = 1024
GRID_W = 64
HG_HEADS = 4
HG_D = 128
HG_W = HG_HEADS * HG_D
CHUNK = 64
SUB = 8
FAST_SUB = 16
MAX_BLOCK_DECAY_LOG2 = 80.0
NA_HEADS = 8
NA_DH = 64
NA_W = NA_HEADS * NA_DH
NA_KH = 8
NA_KW = 16
D_FF = 2816
EPS = 1e-6
MASK_VALUE = -1e30
LOG2_E = 1.4426950408889634

TOKEN_TILE = 512
HGRN_TILE = 1024
INPROJ_TILE = 1024
HGRN_SEQS_PER_STEP = 1
NA_TILE_ROWS = 16
NA_SPAN_ROWS = NA_TILE_ROWS + NA_KH
VMEM_LIMIT_BYTES = 56 * 1024 * 1024


def _sigmoid(x):
    return 0.5 * jnp.tanh(0.5 * x) + 0.5


def _silu(x):
    return x * _sigmoid(x)


def _const_spec(shape):
    return pl.BlockSpec(shape, lambda *_: (0,) * len(shape), pipeline_mode=pl.Buffered(1))


def _chunk_scan(g, rowmod, reverse):
    n = g.shape[0]
    k = 1
    while k < CHUNK:
        if reverse:
            shifted = pltpu.roll(g, n - k, 0)
            g = g + jnp.where(rowmod < CHUNK - k, shifted, 0.0)
        else:
            shifted = pltpu.roll(g, k, 0)
            g = g + jnp.where(rowmod >= k, shifted, 0.0)
        k *= 2
    return g


N_PROJ_BLOCKS = 8


def _inproj_kernel(x_ref, nw_ref, qnw_ref, knw_ref, gmat_ref, *refs):
    w_refs = refs[:N_PROJ_BLOCKS]
    q_ref, ff_ref, fb_ref, v_ref, sg_ref, nq_ref, nk_ref, nv_ref = refs[N_PROJ_BLOCKS:]
    halves = [slice(0, INPROJ_TILE // 2), slice(INPROJ_TILE // 2, INPROJ_TILE)]
    hs = []
    for r in halves:
        x = x_ref[0, r]
        ms = jnp.mean(x * x, axis=-1, keepdims=True)
        hs.append((x * lax.rsqrt(ms + EPS) * nw_ref[...]).astype(BF16))

    def head_norm(a, w):
        ssum = jnp.dot((a * a).astype(BF16), gmat_ref[...], preferred_element_type=F32)
        return a * lax.rsqrt(ssum * (1.0 / NA_DH) + EPS) * w

    post = [
        (q_ref, lambda a: a),
        (ff_ref, lambda a: a),
        (fb_ref, lambda a: a),
        (v_ref, lambda a: a),
        (sg_ref, _silu),
        (nq_ref, lambda a: head_norm(a, qnw_ref[...]) * (NA_DH ** -0.5 * LOG2_E)),
        (nk_ref, lambda a: head_norm(a, knw_ref[...])),
        (nv_ref, lambda a: a),
    ]
    for w_ref, (ref, fn) in zip(w_refs, post):
        for r, h in zip(halves, hs):
            ref[0, r] = fn(jnp.dot(h, w_ref[...], preferred_element_type=F32)).astype(ref.dtype)


def _inproj(x, norm_w, w_in, qnw, knw, gmat):
    B, T, _ = x.shape
    tm = INPROJ_TILE
    tok = lambda w: pl.BlockSpec((1, tm, w), lambda b, i: (b, i, 0))
    w_block = lambda c: pl.BlockSpec((D_MODEL, HG_W), lambda b, i: (0, c),
                                     pipeline_mode=pl.Buffered(1))
    f32_out = jax.ShapeDtypeStruct((B, T, HG_W), F32)
    bf16_out = jax.ShapeDtypeStruct((B, T, HG_W), BF16)
    return pl.pallas_call(
        _inproj_kernel,
        grid=(B, T // tm),
        in_specs=[tok(D_MODEL), _const_spec((1, D_MODEL)), _const_spec((1, NA_W)),
                  _const_spec((1, NA_W)), _const_spec((NA_W, NA_W))]
                 + [w_block(c) for c in range(N_PROJ_BLOCKS)],
        out_specs=[tok(HG_W)] * 8,
        out_shape=[f32_out, f32_out, f32_out, bf16_out, f32_out, bf16_out, bf16_out, bf16_out],
        compiler_params=pltpu.CompilerParams(
            dimension_semantics=("parallel", "parallel"), vmem_limit_bytes=VMEM_LIMIT_BYTES),
        name="inproj",
    )(x, norm_w, qnw, knw, gmat, *([w_in] * N_PROJ_BLOCKS))


def _nt_dot(a, b):
    return lax.dot_general(a, b, (((1,), (1,)), ((), ())), preferred_element_type=F32)


def _tn_dot(a, b):
    return lax.dot_general(a, b, (((0,), (0,)), ((), ())), preferred_element_type=F32)


def _hgrn_intra(q_ref, kk_ref, cum_ref, v_ref, c, lanes, reverse, exact):
    def ld(ref, lo, n):
        return ref[c, lo:lo + n, lanes]

    def dif(u, w):
        return (w - u) if reverse else (u - w)

    x_ref, y_ref = (kk_ref, q_ref) if reverse else (q_ref, kk_ref)
    q = ld(q_ref, 0, CHUNK)
    kk = ld(kk_ref, 0, CHUNK)
    cum = ld(cum_ref, 0, CHUNK)
    v = v_ref[c, :, lanes]
    X, Y = (kk, q) if reverse else (q, kk)
    tot = ld(cum_ref, 0 if reverse else CHUNK - 1, 1)
    sub_rows = SUB if exact else FAST_SUB
    nblk = CHUNK // sub_rows
    lane = lax.broadcasted_iota(jnp.int32, (sub_rows, CHUNK), 1)
    sub = lax.broadcasted_iota(jnp.int32, (sub_rows, CHUNK), 0)

    if exact:
        blocks = []
        for I in range(nblk):
            lo = I * SUB
            Xb = X[lo:lo + SUB]
            ab = cum[lo:lo + SUB]
            blk = jnp.zeros((SUB, CHUNK), F32)
            for j in range(SUB):
                yj = ld(y_ref, lo + j, 1)
                aj = ld(cum_ref, lo + j, 1)
                col = jnp.sum(Xb * yj * jnp.exp2(dif(ab, aj)), axis=-1, keepdims=True)
                blk = jnp.where(lane == lo + j, col, blk)
            blocks.append(jnp.where(lane - lo <= sub, blk, 0.0))
        first_row_block = 1
    else:
        blocks = [None] * nblk
        first_row_block = 0

    ncol = nblk - first_row_block
    ends = [ld(cum_ref, J * sub_rows + sub_rows - 1, 1) for J in range(ncol)]
    ye = [Y[J * sub_rows:(J + 1) * sub_rows]
          * jnp.exp2(dif(ends[J], cum[J * sub_rows:(J + 1) * sub_rows])) for J in range(ncol)]
    if ncol < nblk:
        ye.append(jnp.zeros((CHUNK - ncol * sub_rows, HG_D), F32))
    ye = jnp.concatenate(ye, axis=0).astype(BF16)
    starts = [(J + first_row_block) * sub_rows for J in range(ncol)]
    xe = [X[starts[J]:] * jnp.exp2(dif(cum[starts[J]:], ends[J])) for J in range(ncol)]
    res = _nt_dot(jnp.concatenate(xe, axis=0).astype(BF16), ye)
    lane_blk = lane // sub_rows
    off = 0
    for J in range(ncol):
        for I in range(J + first_row_block, nblk):
            r0 = off + I * sub_rows - starts[J]
            piece = res[r0:r0 + sub_rows]
            blocks[I] = piece if blocks[I] is None else jnp.where(lane_blk == J, piece, blocks[I])
        off += CHUNK - starts[J]
    if not exact:
        blocks = [jnp.where(lane - I * sub_rows <= sub, blk, 0.0) for I, blk in enumerate(blocks)]
    L = jnp.concatenate(blocks, axis=0).astype(BF16)

    intra = _tn_dot(L, v) if reverse else jnp.dot(L, v, preferred_element_type=F32)
    qe = (q * jnp.exp2(cum)).astype(BF16)
    ke = (kk * jnp.exp2(tot - cum)).astype(BF16)
    decay = jnp.transpose(jnp.broadcast_to(jnp.exp2(tot), (HG_D, HG_D)))
    return intra, _tn_dot(ke, v), qe, decay


def _gate_lower_bound(lb_ref, reverse):
    d = 1 if reverse else 0
    a0 = lb_ref[0, d:d + 1, :]
    a1 = lb_ref[1, d:d + 1, :]
    m = jnp.maximum(a0, a1)
    e0 = jnp.exp(a0 - m)
    e1 = jnp.exp(a1 - m)
    return e0 / (e0 + e1)


def _hgrn_tile(seqs, lb, emit, reverse, nchunk, exact):
    def chunk_of(ci):
        return (nchunk - 1 - ci) if reverse else ci

    row = lax.broadcasted_iota(jnp.int32, (CHUNK, HG_W), 0)

    def intra(ci):
        c = chunk_of(ci)
        for sq in seqs:
            q_ref, kk_ref, cum_ref = sq["gates"]
            o_buf, u_buf, q_buf, d_buf = sq["slots"]
            f = lb + (1.0 - lb) * _sigmoid(sq["fl"][c])
            q_ref[c] = _silu(sq["qr"][c])
            kk_ref[c] = 1.0 - f
            cum_ref[c] = _chunk_scan(jnp.log2(f), row, reverse)
            for h in range(HG_HEADS):
                lanes = slice(h * HG_D, (h + 1) * HG_D)
                o, u, qe, dec = _hgrn_intra(q_ref, kk_ref, cum_ref, sq["v"], c, lanes, reverse, exact)
                o_buf[ci, h] = o
                u_buf[ci, h] = u
                q_buf[ci, h] = qe
                d_buf[ci, h] = dec

    def inter(ci):
        for sq in seqs:
            o_buf, u_buf, q_buf, d_buf = sq["slots"]
            st_ref = sq["st"]
            for h in range(HG_HEADS):
                lanes = slice(h * HG_D, (h + 1) * HG_D)
                st = st_ref[h]
                emit(sq, chunk_of(ci), lanes,
                     o_buf[ci, h] + jnp.dot(q_buf[ci, h], st.astype(BF16), preferred_element_type=F32))
                st_ref[h] = st * d_buf[ci, h] + u_buf[ci, h]

    intra(0)
    for ci in range(nchunk):
        if ci + 1 < nchunk:
            intra(ci + 1)
        inter(ci)


def _gated_norm(out, of_ref, sg_ref, gw_ref, c, lanes):
    o = out + of_ref[c, :, lanes]
    ms = jnp.mean(o * o, axis=-1, keepdims=True)
    return o * lax.rsqrt(ms + EPS) * gw_ref[...] * sg_ref[c, :, lanes]


def _hgrn_kernel(*refs, reverse, tt):
    if reverse:
        qr_ref, fl_ref, v_ref, lb_ref, of_ref, sg_ref, gw_ref, o_ref = refs[:8]
    else:
        qr_ref, fl_ref, v_ref, lb_ref, o_ref = refs[:5]
    st_ref = refs[-8]
    gates = refs[-7:-4]
    slots = refs[-4:]

    @pl.when(pl.program_id(1) == 0)
    def _():
        st_ref[...] = jnp.zeros_like(st_ref)

    seqs = []
    for s in range(HGRN_SEQS_PER_STEP):
        sq = {"qr": qr_ref.at[s], "fl": fl_ref.at[s], "v": v_ref.at[s], "st": st_ref.at[s],
              "gates": [g.at[s] for g in gates], "slots": [b.at[s] for b in slots],
              "out": o_ref.at[s]}
        if reverse:
            sq["of"] = of_ref.at[s]
            sq["sg"] = sg_ref.at[s]
        seqs.append(sq)

    def emit(sq, c, lanes, out):
        if reverse:
            out = _gated_norm(out, sq["of"], sq["sg"], gw_ref, c, lanes)
        sq["out"][c, :, lanes] = out.astype(o_ref.dtype)

    nchunk = tt // CHUNK
    lb = _gate_lower_bound(lb_ref, reverse)
    bounded = jnp.min(lb) >= 2.0 ** (-MAX_BLOCK_DECAY_LOG2 / (FAST_SUB - 1))

    @pl.when(bounded)
    def _():
        _hgrn_tile(seqs, lb, emit, reverse, nchunk, exact=False)

    @pl.when(jnp.logical_not(bounded))
    def _():
        _hgrn_tile(seqs, lb, emit, reverse, nchunk, exact=True)


def _hgrn(q_raw, f_logit, v, lb_param, reverse, o_fwd=None, sg=None, gnorm_w=None):
    B, T, _ = q_raw.shape
    tt = HGRN_TILE
    nt = T // tt
    ns = HGRN_SEQS_PER_STEP
    assert B % ns == 0
    if reverse:
        tile = lambda b, i: (b, nt - 1 - i, 0, 0)
    else:
        tile = lambda b, i: (b, i, 0, 0)
    chunked = lambda a: a.reshape(B, T // CHUNK, CHUNK, HG_W)
    tok = pl.BlockSpec((ns, tt // CHUNK, CHUNK, HG_W), tile)
    args = [chunked(q_raw), chunked(f_logit), chunked(v), lb_param]
    in_specs = [tok, tok, tok, _const_spec(lb_param.shape)]
    if reverse:
        args += [chunked(o_fwd), chunked(sg), gnorm_w]
        in_specs += [tok, tok, _const_spec((1, HG_D))]
    out = pl.pallas_call(
        functools.partial(_hgrn_kernel, reverse=reverse, tt=tt),
        grid=(B // ns, nt),
        in_specs=in_specs,
        out_specs=tok,
        out_shape=jax.ShapeDtypeStruct((B, T // CHUNK, CHUNK, HG_W), BF16 if reverse else F32),
        scratch_shapes=_hgrn_scratch(ns, tt // CHUNK),
        compiler_params=pltpu.CompilerParams(
            dimension_semantics=("parallel", "arbitrary"), vmem_limit_bytes=VMEM_LIMIT_BYTES),
        name="hgrn_bwd" if reverse else "hgrn_fwd",
    )(*args)
    return out.reshape(B, T, HG_W)


def _hgrn_scratch(ns, nchunk):
    return [pltpu.VMEM((ns, HG_HEADS, HG_D, HG_D), F32)] + 3 * [
            pltpu.VMEM((ns, nchunk, CHUNK, HG_W), F32)] + [
            pltpu.VMEM((ns, nchunk, HG_HEADS, CHUNK, HG_D), F32),
            pltpu.VMEM((ns, nchunk, HG_HEADS, HG_D, HG_D), F32),
            pltpu.VMEM((ns, nchunk, HG_HEADS, CHUNK, HG_D), BF16),
            pltpu.VMEM((ns, nchunk, HG_HEADS, HG_D, HG_D), F32)]


def _na_bias_table(rpb):
    q = np.arange(GRID_W)
    kc = np.arange(GRID_W)
    win_start = np.clip(q - NA_KW // 2, 0, GRID_W - NA_KW)
    valid = (kc[None, :] >= win_start[:, None]) & (kc[None, :] < win_start[:, None] + NA_KW)
    dc = np.clip(kc[None, :] - q[:, None], -(NA_KW - 1), NA_KW - 1) + NA_KW - 1
    n_c = 2 * NA_KW - 1
    onehot = np.zeros((GRID_W, 2, GRID_W, 2, n_c), np.float32)
    for half in range(2):
        onehot[q[:, None], half, kc[None, :], half, dc] = 1.0
    onehot = jnp.asarray(onehot.reshape(GRID_W, 2 * GRID_W, 2 * n_c))
    valid2 = np.tile(valid, (1, 2))
    n_e = 2 * NA_KH - 2
    rows2 = jnp.stack([rpb[:, 0:n_e], rpb[:, 1:n_e + 1]], axis=2).astype(F32)
    rows2 = rows2.reshape(NA_HEADS // 2, 2, n_e, 2 * n_c).transpose(0, 2, 1, 3)
    tab = jnp.einsum("pexz,qyz->pexqy", rows2, onehot, precision=lax.Precision.HIGHEST)
    tab = jnp.where(valid2[None, None, None], tab * LOG2_E, MASK_VALUE)
    return tab.reshape(NA_HEADS // 2, n_e, 2 * GRID_W, 2 * GRID_W)


def _na_span_start(i, rows):
    return jnp.clip(i * NA_TILE_ROWS - NA_KH // 2, 0, rows - NA_SPAN_ROWS)


def _na_kernel(q_ref, k_ref, v_ref, bias_ref, o_ref, sbuf0, sbuf1, *, rows):
    i = pl.program_id(1)
    lane = lax.broadcasted_iota(jnp.int32, (GRID_W, 2 * NA_DH), 1)
    first = lane < NA_DH
    npairs = NA_HEADS // 2
    span_start = _na_span_start(i, rows)

    def window(j):
        r = i * NA_TILE_ROWS + j
        start = jnp.clip(r - NA_KH // 2, 0, rows - NA_KH)
        koff = pl.multiple_of((start - span_start) * GRID_W, GRID_W)
        return r - start, pl.ds(koff, NA_KH * GRID_W), pl.ds(pl.multiple_of(j * GRID_W, GRID_W), GRID_W)

    def scores(j, sbuf):
        shift, krows, qrows = window(j)
        for p in range(npairs):
            lanes = slice(p * 2 * NA_DH, (p + 1) * 2 * NA_DH)
            qp = q_ref[0, qrows, lanes]
            zero = jnp.zeros_like(qp)
            qab = jnp.concatenate([jnp.where(first, qp, zero), jnp.where(first, zero, qp)], axis=0)
            s = _nt_dot(qab, k_ref[0, krows, lanes])
            for m in range(NA_KH // 2):
                cols = slice(m * 2 * GRID_W, (m + 1) * 2 * GRID_W)
                sbuf[p, :, cols] = s[:, cols] + bias_ref[p, NA_KH - 1 - shift + 2 * m]

    def attend(j, sbuf):
        _, krows, qrows = window(j)
        for p in range(npairs):
            lanes = slice(p * 2 * NA_DH, (p + 1) * 2 * NA_DH)
            s = sbuf[p]
            mx = jnp.max(s, axis=-1, keepdims=True)
            e = jnp.exp2(s - mx)
            den = jnp.sum(e, axis=-1, keepdims=True)
            pv = jnp.dot(e.astype(BF16), v_ref[0, krows, lanes], preferred_element_type=F32) / den
            o_ref[0, qrows, lanes] = jnp.where(first, pv[:GRID_W], pv[GRID_W:]).astype(o_ref.dtype)

    sbufs = (sbuf0, sbuf1)
    scores(0, sbufs[0])
    for j in range(NA_TILE_ROWS):
        if j + 1 < NA_TILE_ROWS:
            scores(j + 1, sbufs[(j + 1) % 2])
        attend(j, sbufs[j % 2])


def _na(nq, nk, nv, bias):
    B, T, _ = nq.shape
    rows = T // GRID_W
    tile_tokens = NA_TILE_ROWS * GRID_W
    nt = rows // NA_TILE_ROWS
    cur = pl.BlockSpec((1, tile_tokens, NA_W), lambda b, i: (b, i, 0))
    span = pl.BlockSpec((pl.Element(1), pl.Element(NA_SPAN_ROWS * GRID_W), pl.Element(NA_W)),
                        lambda b, i: (b, _na_span_start(i, rows) * GRID_W, 0))
    return pl.pallas_call(
        functools.partial(_na_kernel, rows=rows),
        grid=(B, nt),
        in_specs=[cur, span, span, _const_spec(bias.shape)],
        out_specs=cur,
        out_shape=jax.ShapeDtypeStruct((B, T, NA_W), BF16),
        scratch_shapes=[pltpu.VMEM((NA_HEADS // 2, 2 * GRID_W, NA_KH * GRID_W), F32),
                        pltpu.VMEM((NA_HEADS // 2, 2 * GRID_W, NA_KH * GRID_W), F32)],
        compiler_params=pltpu.CompilerParams(
            dimension_semantics=("parallel", "parallel"), vmem_limit_bytes=VMEM_LIMIT_BYTES),
        name="natten",
    )(nq, nk, nv, bias)


def _ffn_kernel(x_ref, oh_ref, on_ref, wo_ref, nw_ref, wg_ref, wu_ref, wd_ref, y_ref):
    halves = [slice(0, TOKEN_TILE // 2), slice(TOKEN_TILE // 2, TOKEN_TILE)]
    xs = [x_ref[0, r] + jnp.dot(oh_ref[0, r], wo_ref[:HG_W], preferred_element_type=F32)
          + jnp.dot(on_ref[0, r], wo_ref[HG_W:], preferred_element_type=F32) for r in halves]
    hs = [(x * lax.rsqrt(jnp.mean(x * x, axis=-1, keepdims=True) + EPS) * nw_ref[...]).astype(BF16)
          for x in xs]
    gates = [jnp.dot(h, wg_ref[...], preferred_element_type=F32) for h in hs]
    ups = [jnp.dot(h, wu_ref[...], preferred_element_type=F32) for h in hs]
    acts = [(_silu(g) * u).astype(BF16) for g, u in zip(gates, ups)]
    for r, x, act in zip(halves, xs, acts):
        y_ref[0, r] = x + jnp.dot(act, wd_ref[...], preferred_element_type=F32)


def _ffn(x, o_hgrn, o_na, w_out, norm_w, w_gate, w_up, w_down):
    B, T, _ = x.shape
    tm = TOKEN_TILE
    tok = lambda w: pl.BlockSpec((1, tm, w), lambda b, i: (b, i, 0))
    return pl.pallas_call(
        _ffn_kernel,
        grid=(B, T // tm),
        in_specs=[tok(D_MODEL), tok(HG_W), tok(NA_W), _const_spec(w_out.shape),
                  _const_spec((1, D_MODEL)), _const_spec(w_gate.shape),
                  _const_spec(w_up.shape), _const_spec(w_down.shape)],
        out_specs=tok(D_MODEL),
        out_shape=jax.ShapeDtypeStruct(x.shape, x.dtype),
        compiler_params=pltpu.CompilerParams(
            dimension_semantics=("parallel", "parallel"), vmem_limit_bytes=VMEM_LIMIT_BYTES),
        name="outproj_ffn",
    )(x, o_hgrn, o_na, w_out, norm_w, w_gate, w_up, w_down)


def _encoder_layer(x, p):
    q, ff, fb, v, sg, nq, nk, nv = _inproj(
        x, p["norm_mix_w"], p["w_in"], p["qnw"], p["knw"], p["gmat"])
    o_fwd = _hgrn(q, ff, v, p["lb"], reverse=False)
    o_hgrn = _hgrn(q, fb, v, p["lb"], reverse=True, o_fwd=o_fwd, sg=sg, gnorm_w=p["gnorm_w"])
    o_na = _na(nq, nk, nv, p["na_bias"])
    return _ffn(x, o_hgrn, o_na, p["w_out"], p["norm_ffn_w"], p["w_gate"], p["w_up"], p["w_down"])


def kernel(x_prompt, x_sample, norm_mix_w, w_in, hgrn_lb, hgrn_gnorm_w, na_q_norm_w, na_k_norm_w,
           na_rpb, w_out, norm_ffn_w, w_gate, w_up, w_down):
    head_of_lane = np.arange(NA_W) // NA_DH
    p = {
        "norm_mix_w": norm_mix_w[0][None].astype(F32),
        "w_in": w_in[0].astype(BF16),
        "lb": hgrn_lb.astype(F32),
        "qnw": jnp.tile(na_q_norm_w[0].astype(F32), NA_HEADS)[None],
        "knw": jnp.tile(na_k_norm_w[0].astype(F32), NA_HEADS)[None],
        "gmat": jnp.asarray(head_of_lane[:, None] == head_of_lane[None, :], dtype=BF16),
        "gnorm_w": hgrn_gnorm_w[0][None].astype(F32),
        "na_bias": _na_bias_table(na_rpb[0]),
        "w_out": w_out[0].astype(BF16),
        "norm_ffn_w": norm_ffn_w[0][None].astype(F32),
        "w_gate": w_gate[0].astype(BF16),
        "w_up": w_up[0].astype(BF16),
        "w_down": w_down[0].astype(BF16),
    }
    return (_encoder_layer(x_prompt, p), _encoder_layer(x_sample, p))
```

```python
import functools

import numpy as np
import jax
import jax.numpy as jnp
from jax import lax
from jax.experimental import pallas as pl
from jax.experimental.pallas import tpu as pltpu

F32 = jnp.float32
BF16 = jnp.bfloat16

D_MODEL = 1024
GRID_W = 64
HG_HEADS = 4
HG_D = 128
HG_W = HG_HEADS * HG_D
CHUNK = 64
SUB = 8
FAST_SUB = 16
MAX_BLOCK_DECAY_LOG2 = 80.0
NA_HEADS = 8
NA_DH = 64
NA_W = NA_HEADS * NA_DH
NA_KH = 8
NA_KW = 16
D_FF = 2816
EPS = 1e-6
MASK_VALUE = -1e30
LOG2_E = 1.4426950408889634

TOKEN_TILE = 512
HGRN_TILE = 1024
INPROJ_TILE = 1024
HGRN_SEQS_PER_STEP = 1
NA_TILE_ROWS = 16
NA_SPAN_ROWS = NA_TILE_ROWS + NA_KH
VMEM_LIMIT_BYTES = 56 * 1024 * 1024


def _sigmoid(x):
    return 0.5 * jnp.tanh(0.5 * x) + 0.5


def _silu(x):
    return x * _sigmoid(x)


def _const_spec(shape):
    return pl.BlockSpec(shape, lambda *_: (0,) * len(shape), pipeline_mode=pl.Buffered(1))


def _chunk_scan(g, rowmod, reverse):
    n = g.shape[0]
    in_group = rowmod % 8
    k = 1
    while k < 8:
        if reverse:
            shifted = pltpu.roll(g, n - k, 0)
            g = g + jnp.where(in_group < 8 - k, shifted, 0.0)
        else:
            shifted = pltpu.roll(g, k, 0)
            g = g + jnp.where(in_group >= k, shifted, 0.0)
        k *= 2
    groups = [g[lo:lo + 8] for lo in range(0, n, 8)]
    per_chunk = CHUNK // 8
    order = range(len(groups) - 1, -1, -1) if reverse else range(len(groups))
    carry = None
    for i in order:
        first_of_chunk = (i % per_chunk == per_chunk - 1) if reverse else (i % per_chunk == 0)
        edge = groups[i][0:1] if reverse else groups[i][7:8]
        if first_of_chunk:
            carry = edge
        else:
            groups[i] = groups[i] + carry
            carry = carry + edge
    return jnp.concatenate(groups, axis=0)


N_PROJ_BLOCKS = 8


def _inproj_kernel(x_ref, nw_ref, qnw_ref, knw_ref, gmat_ref, *refs):
    w_refs = refs[:N_PROJ_BLOCKS]
    q_ref, ff_ref, fb_ref, v_ref, sg_ref, nq_ref, nk_ref, nv_ref = refs[N_PROJ_BLOCKS:]
    halves = [slice(0, INPROJ_TILE // 2), slice(INPROJ_TILE // 2, INPROJ_TILE)]
    hs = []
    for r in halves:
        x = x_ref[0, r]
        ms = jnp.mean(x * x, axis=-1, keepdims=True)
        hs.append((x * lax.rsqrt(ms + EPS) * nw_ref[...]).astype(BF16))

    def head_norm(a, w):
        ssum = jnp.dot((a * a).astype(BF16), gmat_ref[...], preferred_element_type=F32)
        return a * lax.rsqrt(ssum * (1.0 / NA_DH) + EPS) * w

    post = [
        (q_ref, lambda a: a),
        (ff_ref, lambda a: a),
        (fb_ref, lambda a: a),
        (v_ref, lambda a: a),
        (sg_ref, _silu),
        (nq_ref, lambda a: head_norm(a, qnw_ref[...]) * (NA_DH ** -0.5 * LOG2_E)),
        (nk_ref, lambda a: head_norm(a, knw_ref[...])),
        (nv_ref, lambda a: a),
    ]
    for w_ref, (ref, fn) in zip(w_refs, post):
        for r, h in zip(halves, hs):
            ref[0, r] = fn(jnp.dot(h, w_ref[...], preferred_element_type=F32)).astype(ref.dtype)


def _inproj(x, norm_w, w_in, qnw, knw, gmat):
    B, T, _ = x.shape
    tm = INPROJ_TILE
    tok = lambda w: pl.BlockSpec((1, tm, w), lambda b, i: (b, i, 0))
    w_block = lambda c: pl.BlockSpec((D_MODEL, HG_W), lambda b, i: (0, c),
                                     pipeline_mode=pl.Buffered(1))
    f32_out = jax.ShapeDtypeStruct((B, T, HG_W), F32)
    bf16_out = jax.ShapeDtypeStruct((B, T, HG_W), BF16)
    return pl.pallas_call(
        _inproj_kernel,
        grid=(B, T // tm),
        in_specs=[tok(D_MODEL), _const_spec((1, D_MODEL)), _const_spec((1, NA_W)),
                  _const_spec((1, NA_W)), _const_spec((NA_W, NA_W))]
                 + [w_block(c) for c in range(N_PROJ_BLOCKS)],
        out_specs=[tok(HG_W)] * 8,
        out_shape=[f32_out, f32_out, f32_out, bf16_out, f32_out, bf16_out, bf16_out, bf16_out],
        compiler_params=pltpu.CompilerParams(
            dimension_semantics=("parallel", "parallel"), vmem_limit_bytes=VMEM_LIMIT_BYTES),
        name="inproj",
    )(x, norm_w, qnw, knw, gmat, *([w_in] * N_PROJ_BLOCKS))


def _nt_dot(a, b):
    return lax.dot_general(a, b, (((1,), (1,)), ((), ())), preferred_element_type=F32)


def _tn_dot(a, b):
    return lax.dot_general(a, b, (((0,), (0,)), ((), ())), preferred_element_type=F32)


def _hgrn_intra(q_ref, kk_ref, cum_ref, v_ref, c, lanes, reverse, exact):
    def ld(ref, lo, n):
        return ref[c, lo:lo + n, lanes]

    def dif(u, w):
        return (w - u) if reverse else (u - w)

    x_ref, y_ref = (kk_ref, q_ref) if reverse else (q_ref, kk_ref)
    q = ld(q_ref, 0, CHUNK)
    kk = ld(kk_ref, 0, CHUNK)
    cum = ld(cum_ref, 0, CHUNK)
    v = v_ref[c, :, lanes]
    X, Y = (kk, q) if reverse else (q, kk)
    tot = ld(cum_ref, 0 if reverse else CHUNK - 1, 1)
    sub_rows = SUB if exact else FAST_SUB
    nblk = CHUNK // sub_rows
    lane = lax.broadcasted_iota(jnp.int32, (sub_rows, CHUNK), 1)
    sub = lax.broadcasted_iota(jnp.int32, (sub_rows, CHUNK), 0)

    if exact:
        blocks = []
        for I in range(nblk):
            lo = I * SUB
            Xb = X[lo:lo + SUB]
            ab = cum[lo:lo + SUB]
            blk = jnp.zeros((SUB, CHUNK), F32)
            for j in range(SUB):
                yj = ld(y_ref, lo + j, 1)
                aj = ld(cum_ref, lo + j, 1)
                col = jnp.sum(Xb * yj * jnp.exp2(dif(ab, aj)), axis=-1, keepdims=True)
                blk = jnp.where(lane == lo + j, col, blk)
            blocks.append(jnp.where(lane - lo <= sub, blk, 0.0))
        first_row_block = 1
    else:
        blocks = [None] * nblk
        first_row_block = 0

    ncol = nblk - first_row_block
    ends = [ld(cum_ref, J * sub_rows + sub_rows - 1, 1) for J in range(ncol)]
    ye = [Y[J * sub_rows:(J + 1) * sub_rows]
          * jnp.exp2(dif(ends[J], cum[J * sub_rows:(J + 1) * sub_rows])) for J in range(ncol)]
    if ncol < nblk:
        ye.append(jnp.zeros((CHUNK - ncol * sub_rows, HG_D), F32))
    ye = jnp.concatenate(ye, axis=0).astype(BF16)
    starts = [(J + first_row_block) * sub_rows for J in range(ncol)]
    xe = [X[starts[J]:] * jnp.exp2(dif(cum[starts[J]:], ends[J])) for J in range(ncol)]
    res = _nt_dot(jnp.concatenate(xe, axis=0).astype(BF16), ye)
    lane_blk = lane // sub_rows
    off = 0
    for J in range(ncol):
        for I in range(J + first_row_block, nblk):
            r0 = off + I * sub_rows - starts[J]
            piece = res[r0:r0 + sub_rows]
            blocks[I] = piece if blocks[I] is None else jnp.where(lane_blk == J, piece, blocks[I])
        off += CHUNK - starts[J]
    if not exact:
        blocks = [jnp.where(lane - I * sub_rows <= sub, blk, 0.0) for I, blk in enumerate(blocks)]
    L = jnp.concatenate(blocks, axis=0).astype(BF16)

    intra = _tn_dot(L, v) if reverse else jnp.dot(L, v, preferred_element_type=F32)
    qe = (q * jnp.exp2(cum)).astype(BF16)
    ke = (kk * jnp.exp2(tot - cum)).astype(BF16)
    decay = jnp.transpose(jnp.broadcast_to(jnp.exp2(tot), (HG_D, HG_D)))
    return intra, _tn_dot(ke, v), qe, decay


def _gate_lower_bound(lb_ref, reverse):
    d = 1 if reverse else 0
    a0 = lb_ref[0, d:d + 1, :]
    a1 = lb_ref[1, d:d + 1, :]
    m = jnp.maximum(a0, a1)
    e0 = jnp.exp(a0 - m)
    e1 = jnp.exp(a1 - m)
    return e0 / (e0 + e1)


def _hgrn_tile(seqs, lb, emit, reverse, nchunk, exact):
    def chunk_of(ci):
        return (nchunk - 1 - ci) if reverse else ci

    row = lax.broadcasted_iota(jnp.int32, (CHUNK, HG_W), 0)

    def intra(ci):
        c = chunk_of(ci)
        for sq in seqs:
            q_ref, kk_ref, cum_ref = sq["gates"]
            o_buf, u_buf, q_buf, d_buf = sq["slots"]
            f = lb + (1.0 - lb) * _sigmoid(sq["fl"][c])
            q_ref[c] = _silu(sq["qr"][c])
            kk_ref[c] = 1.0 - f
            cum_ref[c] = _chunk_scan(jnp.log2(f), row, reverse)
            for h in range(HG_HEADS):
                lanes = slice(h * HG_D, (h + 1) * HG_D)
                o, u, qe, dec = _hgrn_intra(q_ref, kk_ref, cum_ref, sq["v"], c, lanes, reverse, exact)
                o_buf[ci, h] = o
                u_buf[ci, h] = u
                q_buf[ci, h] = qe
                d_buf[ci, h] = dec

    def inter(ci):
        for sq in seqs:
            o_buf, u_buf, q_buf, d_buf = sq["slots"]
            st_ref = sq["st"]
            for h in range(HG_HEADS):
                lanes = slice(h * HG_D, (h + 1) * HG_D)
                st = st_ref[h]
                emit(sq, chunk_of(ci), lanes,
                     o_buf[ci, h] + jnp.dot(q_buf[ci, h], st.astype(BF16), preferred_element_type=F32))
                st_ref[h] = st * d_buf[ci, h] + u_buf[ci, h]

    intra(0)
    for ci in range(nchunk):
        if ci + 1 < nchunk:
            intra(ci + 1)
        inter(ci)


def _gated_norm(out, of_ref, sg_ref, gw_ref, c, lanes):
    o = out + of_ref[c, :, lanes]
    ms = jnp.mean(o * o, axis=-1, keepdims=True)
    return o * lax.rsqrt(ms + EPS) * gw_ref[...] * sg_ref[c, :, lanes]


def _hgrn_kernel(*refs, reverse, tt):
    if reverse:
        qr_ref, fl_ref, v_ref, lb_ref, of_ref, sg_ref, gw_ref, o_ref = refs[:8]
    else:
        qr_ref, fl_ref, v_ref, lb_ref, o_ref = refs[:5]
    st_ref = refs[-8]
    gates = refs[-7:-4]
    slots = refs[-4:]

    @pl.when(pl.program_id(1) == 0)
    def _():
        st_ref[...] = jnp.zeros_like(st_ref)

    seqs = []
    for s in range(HGRN_SEQS_PER_STEP):
        sq = {"qr": qr_ref.at[s], "fl": fl_ref.at[s], "v": v_ref.at[s], "st": st_ref.at[s],
              "gates": [g.at[s] for g in gates], "slots": [b.at[s] for b in slots],
              "out": o_ref.at[s]}
        if reverse:
            sq["of"] = of_ref.at[s]
            sq["sg"] = sg_ref.at[s]
        seqs.append(sq)

    def emit(sq, c, lanes, out):
        if reverse:
            out = _gated_norm(out, sq["of"], sq["sg"], gw_ref, c, lanes)
        sq["out"][c, :, lanes] = out.astype(o_ref.dtype)

    nchunk = tt // CHUNK
    lb = _gate_lower_bound(lb_ref, reverse)
    bounded = jnp.min(lb) >= 2.0 ** (-MAX_BLOCK_DECAY_LOG2 / (FAST_SUB - 1))

    @pl.when(bounded)
    def _():
        _hgrn_tile(seqs, lb, emit, reverse, nchunk, exact=False)

    @pl.when(jnp.logical_not(bounded))
    def _():
        _hgrn_tile(seqs, lb, emit, reverse, nchunk, exact=True)


def _hgrn(q_raw, f_logit, v, lb_param, reverse, o_fwd=None, sg=None, gnorm_w=None):
    B, T, _ = q_raw.shape
    tt = HGRN_TILE
    nt = T // tt
    ns = HGRN_SEQS_PER_STEP
    assert B % ns == 0
    if reverse:
        tile = lambda b, i: (b, nt - 1 - i, 0, 0)
    else:
        tile = lambda b, i: (b, i, 0, 0)
    chunked = lambda a: a.reshape(B, T // CHUNK, CHUNK, HG_W)
    tok = pl.BlockSpec((ns, tt // CHUNK, CHUNK, HG_W), tile)
    args = [chunked(q_raw), chunked(f_logit), chunked(v), lb_param]
    in_specs = [tok, tok, tok, _const_spec(lb_param.shape)]
    if reverse:
        args += [chunked(o_fwd), chunked(sg), gnorm_w]
        in_specs += [tok, tok, _const_spec((1, HG_D))]
    out = pl.pallas_call(
        functools.partial(_hgrn_kernel, reverse=reverse, tt=tt),
        grid=(B // ns, nt),
        in_specs=in_specs,
        out_specs=tok,
        out_shape=jax.ShapeDtypeStruct((B, T // CHUNK, CHUNK, HG_W), BF16 if reverse else F32),
        scratch_shapes=_hgrn_scratch(ns, tt // CHUNK),
        compiler_params=pltpu.CompilerParams(
            dimension_semantics=("parallel", "arbitrary"), vmem_limit_bytes=VMEM_LIMIT_BYTES),
        name="hgrn_bwd" if reverse else "hgrn_fwd",
    )(*args)
    return out.reshape(B, T, HG_W)


def _hgrn_scratch(ns, nchunk):
    return [pltpu.VMEM((ns, HG_HEADS, HG_D, HG_D), F32)] + 3 * [
            pltpu.VMEM((ns, nchunk, CHUNK, HG_W), F32)] + [
            pltpu.VMEM((ns, nchunk, HG_HEADS, CHUNK, HG_D), F32),
            pltpu.VMEM((ns, nchunk, HG_HEADS, HG_D, HG_D), F32),
            pltpu.VMEM((ns, nchunk, HG_HEADS, CHUNK, HG_D), BF16),
            pltpu.VMEM((ns, nchunk, HG_HEADS, HG_D, HG_D), F32)]


def _na_bias_table(rpb):
    q = np.arange(GRID_W)
    kc = np.arange(GRID_W)
    win_start = np.clip(q - NA_KW // 2, 0, GRID_W - NA_KW)
    valid = (kc[None, :] >= win_start[:, None]) & (kc[None, :] < win_start[:, None] + NA_KW)
    dc = np.clip(kc[None, :] - q[:, None], -(NA_KW - 1), NA_KW - 1) + NA_KW - 1
    n_c = 2 * NA_KW - 1
    onehot = np.zeros((GRID_W, 2, GRID_W, 2, n_c), np.float32)
    for half in range(2):
        onehot[q[:, None], half, kc[None, :], half, dc] = 1.0
    onehot = jnp.asarray(onehot.reshape(GRID_W, 2 * GRID_W, 2 * n_c))
    valid2 = np.tile(valid, (1, 2))
    n_e = 2 * NA_KH - 2
    rows2 = jnp.stack([rpb[:, 0:n_e], rpb[:, 1:n_e + 1]], axis=2).astype(F32)
    rows2 = rows2.reshape(NA_HEADS // 2, 2, n_e, 2 * n_c).transpose(0, 2, 1, 3)
    tab = jnp.einsum("pexz,qyz->pexqy", rows2, onehot, precision=lax.Precision.HIGHEST)
    tab = jnp.where(valid2[None, None, None], tab * LOG2_E, MASK_VALUE)
    return tab.reshape(NA_HEADS // 2, n_e, 2 * GRID_W, 2 * GRID_W)


def _na_span_start(i, rows):
    return jnp.clip(i * NA_TILE_ROWS - NA_KH // 2, 0, rows - NA_SPAN_ROWS)


def _na_kernel(q_ref, k_ref, v_ref, bias_ref, o_ref, sbuf0, sbuf1, *, rows):
    i = pl.program_id(1)
    lane = lax.broadcasted_iota(jnp.int32, (GRID_W, 2 * NA_DH), 1)
    first = lane < NA_DH
    npairs = NA_HEADS // 2
    span_start = _na_span_start(i, rows)

    def window(j):
        r = i * NA_TILE_ROWS + j
        start = jnp.clip(r - NA_KH // 2, 0, rows - NA_KH)
        koff = pl.multiple_of((start - span_start) * GRID_W, GRID_W)
        return r - start, pl.ds(koff, NA_KH * GRID_W), pl.ds(pl.multiple_of(j * GRID_W, GRID_W), GRID_W)

    def scores(j, sbuf):
        shift, krows, qrows = window(j)
        for p in range(npairs):
            lanes = slice(p * 2 * NA_DH, (p + 1) * 2 * NA_DH)
            qp = q_ref[0, qrows, lanes]
            zero = jnp.zeros_like(qp)
            qab = jnp.concatenate([jnp.where(first, qp, zero), jnp.where(first, zero, qp)], axis=0)
            s = _nt_dot(qab, k_ref[0, krows, lanes])
            for m in range(NA_KH // 2):
                cols = slice(m * 2 * GRID_W, (m + 1) * 2 * GRID_W)
                sbuf[p, :, cols] = s[:, cols] + bias_ref[p, NA_KH - 1 - shift + 2 * m]

    def attend(j, sbuf):
        _, krows, qrows = window(j)
        for p in range(npairs):
            lanes = slice(p * 2 * NA_DH, (p + 1) * 2 * NA_DH)
            s = sbuf[p]
            mx = jnp.max(s, axis=-1, keepdims=True)
            e = jnp.exp2(s - mx)
            den = jnp.sum(e, axis=-1, keepdims=True)
            pv = jnp.dot(e.astype(BF16), v_ref[0, krows, lanes], preferred_element_type=F32) / den
            o_ref[0, qrows, lanes] = jnp.where(first, pv[:GRID_W], pv[GRID_W:]).astype(o_ref.dtype)

    sbufs = (sbuf0, sbuf1)
    scores(0, sbufs[0])
    for j in range(NA_TILE_ROWS):
        if j + 1 < NA_TILE_ROWS:
            scores(j + 1, sbufs[(j + 1) % 2])
        attend(j, sbufs[j % 2])


def _na(nq, nk, nv, bias):
    B, T, _ = nq.shape
    rows = T // GRID_W
    tile_tokens = NA_TILE_ROWS * GRID_W
    nt = rows // NA_TILE_ROWS
    cur = pl.BlockSpec((1, tile_tokens, NA_W), lambda b, i: (b, i, 0))
    span = pl.BlockSpec((pl.Element(1), pl.Element(NA_SPAN_ROWS * GRID_W), pl.Element(NA_W)),
                        lambda b, i: (b, _na_span_start(i, rows) * GRID_W, 0))
    return pl.pallas_call(
        functools.partial(_na_kernel, rows=rows),
        grid=(B, nt),
        in_specs=[cur, span, span, _const_spec(bias.shape)],
        out_specs=cur,
        out_shape=jax.ShapeDtypeStruct((B, T, NA_W), BF16),
        scratch_shapes=[pltpu.VMEM((NA_HEADS // 2, 2 * GRID_W, NA_KH * GRID_W), F32),
                        pltpu.VMEM((NA_HEADS // 2, 2 * GRID_W, NA_KH * GRID_W), F32)],
        compiler_params=pltpu.CompilerParams(
            dimension_semantics=("parallel", "parallel"), vmem_limit_bytes=VMEM_LIMIT_BYTES),
        name="natten",
    )(nq, nk, nv, bias)


def _ffn_kernel(x_ref, oh_ref, on_ref, wo_ref, nw_ref, wg_ref, wu_ref, wd_ref, y_ref):
    halves = [slice(0, TOKEN_TILE // 2), slice(TOKEN_TILE // 2, TOKEN_TILE)]
    xs = [x_ref[0, r] + jnp.dot(oh_ref[0, r], wo_ref[:HG_W], preferred_element_type=F32)
          + jnp.dot(on_ref[0, r], wo_ref[HG_W:], preferred_element_type=F32) for r in halves]
    hs = [(x * lax.rsqrt(jnp.mean(x * x, axis=-1, keepdims=True) + EPS) * nw_ref[...]).astype(BF16)
          for x in xs]
    gates = [jnp.dot(h, wg_ref[...], preferred_element_type=F32) for h in hs]
    ups = [jnp.dot(h, wu_ref[...], preferred_element_type=F32) for h in hs]
    acts = [(_silu(g) * u).astype(BF16) for g, u in zip(gates, ups)]
    for r, x, act in zip(halves, xs, acts):
        y_ref[0, r] = x + jnp.dot(act, wd_ref[...], preferred_element_type=F32)


def _ffn(x, o_hgrn, o_na, w_out, norm_w, w_gate, w_up, w_down):
    B, T, _ = x.shape
    tm = TOKEN_TILE
    tok = lambda w: pl.BlockSpec((1, tm, w), lambda b, i: (b, i, 0))
    return pl.pallas_call(
        _ffn_kernel,
        grid=(B, T // tm),
        in_specs=[tok(D_MODEL), tok(HG_W), tok(NA_W), _const_spec(w_out.shape),
                  _const_spec((1, D_MODEL)), _const_spec(w_gate.shape),
                  _const_spec(w_up.shape), _const_spec(w_down.shape)],
        out_specs=tok(D_MODEL),
        out_shape=jax.ShapeDtypeStruct(x.shape, x.dtype),
        compiler_params=pltpu.CompilerParams(
            dimension_semantics=("parallel", "parallel"), vmem_limit_bytes=VMEM_LIMIT_BYTES),
        name="outproj_ffn",
    )(x, o_hgrn, o_na, w_out, norm_w, w_gate, w_up, w_down)


def _encoder_layer(x, p):
    q, ff, fb, v, sg, nq, nk, nv = _inproj(
        x, p["norm_mix_w"], p["w_in"], p["qnw"], p["knw"], p["gmat"])
    o_fwd = _hgrn(q, ff, v, p["lb"], reverse=False)
    o_hgrn = _hgrn(q, fb, v, p["lb"], reverse=True, o_fwd=o_fwd, sg=sg, gnorm_w=p["gnorm_w"])
    o_na = _na(nq, nk, nv, p["na_bias"])
    return _ffn(x, o_hgrn, o_na, p["w_out"], p["norm_ffn_w"], p["w_gate"], p["w_up"], p["w_down"])


def kernel(x_prompt, x_sample, norm_mix_w, w_in, hgrn_lb, hgrn_gnorm_w, na_q_norm_w, na_k_norm_w,
           na_rpb, w_out, norm_ffn_w, w_gate, w_up, w_down):
    head_of_lane = np.arange(NA_W) // NA_DH
    p = {
        "norm_mix_w": norm_mix_w[0][None].astype(F32),
        "w_in": w_in[0].astype(BF16),
        "lb": hgrn_lb.astype(F32),
        "qnw": jnp.tile(na_q_norm_w[0].astype(F32), NA_HEADS)[None],
        "knw": jnp.tile(na_k_norm_w[0].astype(F32), NA_HEADS)[None],
        "gmat": jnp.asarray(head_of_lane[:, None] == head_of_lane[None, :], dtype=BF16),
        "gnorm_w": hgrn_gnorm_w[0][None].astype(F32),
        "na_bias": _na_bias_table(na_rpb[0]),
        "w_out": w_out[0].astype(BF16),
        "norm_ffn_w": norm_ffn_w[0][None].astype(F32),
        "w_gate": w_gate[0].astype(BF16),
        "w_up": w_up[0].astype(BF16),
        "w_down": w_down[0].astype(BF16),
    }
    return (_encoder_layer(x_prompt, p), _encoder_layer(x_sample, p))
```
